```python
import math
import jax, jax.numpy as jnp
from jax import lax
import numpy as np

D_MODEL = 1024
BATCH = 16
SEQ = 2048
DEPTH = 2

CHUNK = 64
Q_BLOCK = 128
EPS = 1e-6
MASK_VALUE = -1e30
MIN_FORGET = 1e-6

D_MIX = D_MODEL
S5_WIDTH = D_MIX // 4
DIFF_WIDTH = 3 * D_MIX // 8
HGRN_WIDTH = D_MIX - S5_WIDTH - DIFF_WIDTH

S5_GROUP = 16
S5_GROUPS = S5_WIDTH // S5_GROUP
S5_STATE = 64
S5_DT_MIN = 1e-3
S5_DT_MAX = 1e-1

DIFF_HEADS = 6
DIFF_VDIM = DIFF_WIDTH // DIFF_HEADS
DIFF_QKDIM = DIFF_VDIM // 2

HGRN_HEADS = 6
HGRN_DIM = HGRN_WIDTH // HGRN_HEADS

D_FF = 2816
CONV_WIDTH = 3

IN_SIZES = (S5_WIDTH,
            DIFF_WIDTH, DIFF_WIDTH, DIFF_WIDTH,
            HGRN_WIDTH, HGRN_WIDTH, HGRN_WIDTH, HGRN_WIDTH)
D_IN = sum(IN_SIZES)

kernel_name = "hybrid_s5_diffattn_hgrn2_convffn"


def rmsnorm(x, g):
    xf = x.astype(jnp.float32)
    r = lax.rsqrt(jnp.mean(xf * xf, axis=-1, keepdims=True) + EPS)
    return (xf * r * g.astype(jnp.float32)).astype(x.dtype)


def split_projection(proj):
    idx = [int(v) for v in np.cumsum(IN_SIZES)[:-1]]
    return jnp.split(proj, idx, axis=-1)


def s5_mixer(u, lam_re, lam_im, log_step, b_re, b_im, c_re, c_im, d, w_glu):
    bsz, s, _ = u.shape
    uf = u.astype(jnp.float32)
    ug = uf.reshape(bsz, s, S5_GROUPS, S5_GROUP)
    lam_re = lam_re.astype(jnp.float32)
    lam_im = lam_im.astype(jnp.float32)
    dt = jnp.exp(log_step.astype(jnp.float32))[:, None]
    mag = jnp.exp(lam_re * dt)
    ang = lam_im * dt
    lb_re, lb_im = mag * jnp.cos(ang), mag * jnp.sin(ang)
    den = lam_re * lam_re + lam_im * lam_im
    num_re, num_im = lb_re - 1.0, lb_im
    coef_re = (num_re * lam_re + num_im * lam_im) / den
    coef_im = (num_im * lam_re - num_re * lam_im) / den
    b_re = b_re.astype(jnp.float32)
    b_im = b_im.astype(jnp.float32)
    bb_re = coef_re[..., None] * b_re - coef_im[..., None] * b_im
    bb_im = coef_re[..., None] * b_im + coef_im[..., None] * b_re
    bu_re = jnp.einsum('gnp,bsgp->bsgn', bb_re, ug)
    bu_im = jnp.einsum('gnp,bsgp->bsgn', bb_im, ug)
    a_re = jnp.broadcast_to(lb_re, bu_re.shape)
    a_im = jnp.broadcast_to(lb_im, bu_im.shape)

    def combine(e1, e2):
        a1r, a1i, b1r, b1i = e1
        a2r, a2i, b2r, b2i = e2
        return (a2r * a1r - a2i * a1i,
                a2r * a1i + a2i * a1r,
                a2r * b1r - a2i * b1i + b2r,
                a2r * b1i + a2i * b1r + b2i)

    _, _, x_re, x_im = lax.associative_scan(combine, (a_re, a_im, bu_re, bu_im), axis=1)
    y = (jnp.einsum('gpn,bsgn->bsgp', c_re.astype(jnp.float32), x_re)
         - jnp.einsum('gpn,bsgn->bsgp', c_im.astype(jnp.float32), x_im))
    y = y.reshape(bsz, s, S5_WIDTH) + d.astype(jnp.float32) * uf
    z = jax.nn.gelu(y)
    out = z * jax.nn.sigmoid(z @ w_glu.astype(jnp.float32))
    return out.astype(u.dtype)


def diff_attention(q, k, v, lam_q1, lam_k1, lam_q2, lam_k2, subln_g, layer_idx):
    bsz, s, _ = q.shape
    q = q.reshape(bsz, s, DIFF_HEADS, 2, DIFF_QKDIM)
    k = k.reshape(bsz, s, DIFF_HEADS, 2, DIFF_QKDIM)
    v = v.reshape(bsz, s, DIFF_HEADS, DIFF_VDIM).astype(jnp.float32)
    lam_init = 0.8 - 0.6 * math.exp(-0.3 * layer_idx)
    lam = (jnp.exp(jnp.sum(lam_q1.astype(jnp.float32) * lam_k1.astype(jnp.float32)))
           - jnp.exp(jnp.sum(lam_q2.astype(jnp.float32) * lam_k2.astype(jnp.float32)))
           + lam_init)
    slopes = 2.0 ** (-8.0 * jnp.arange(1, DIFF_HEADS + 1, dtype=jnp.float32) / DIFF_HEADS)
    scale = DIFF_QKDIM ** -0.5
    k_pos = jnp.arange(s)
    n_blocks = s // Q_BLOCK
    qb = q.reshape(bsz, n_blocks, Q_BLOCK, DIFF_HEADS, 2, DIFF_QKDIM).transpose(1, 0, 2, 3, 4, 5)

    def attend(args):
        q_blk, blk = args
        q_pos = blk * Q_BLOCK + jnp.arange(Q_BLOCK)
        sc = jnp.einsum('bqhcd,bkhcd->bhcqk', q_blk, k).astype(jnp.float32) * scale
        dist = jnp.abs(q_pos[:, None] - k_pos[None, :]).astype(jnp.float32)
        visible = (k_pos[None, :] // CHUNK) <= (q_pos[:, None] // CHUNK)
        bias = jnp.where(visible[None], -slopes[:, None, None] * dist[None], MASK_VALUE)
        p = jax.nn.softmax(sc + bias[None, :, None], axis=-1)
        w = p[:, :, 0] - lam * p[:, :, 1]
        return jnp.einsum('bhqk,bkhd->bqhd', w, v)

    o = lax.map(attend, (qb, jnp.arange(n_blocks)))
    o = o.transpose(1, 0, 2, 3, 4).reshape(bsz, s, DIFF_HEADS, DIFF_VDIM)
    o = rmsnorm(o, subln_g) * (1.0 - lam_init)
    return o.reshape(bsz, s, DIFF_WIDTH).astype(q.dtype)


def hgrn2_mixer(q, f_raw, i, g, lb, norm_g):
    bsz, s, _ = q.shape
    nc = s // CHUNK
    shp = (bsz, s, HGRN_HEADS, HGRN_DIM)
    lb = lb.astype(jnp.float32).reshape(HGRN_HEADS, HGRN_DIM)
    f = lb + (1.0 - lb) * jax.nn.sigmoid(f_raw.astype(jnp.float32).reshape(shp))
    log_f = jnp.log(jnp.maximum(f, MIN_FORGET))
    kk = 1.0 - f
    qq = jax.nn.silu(q.astype(jnp.float32).reshape(shp))
    vv = i.astype(jnp.float32).reshape(shp)

    def to_chunks(t):
        return t.reshape(bsz, nc, CHUNK, HGRN_HEADS, HGRN_DIM).transpose(1, 0, 3, 2, 4)

    qc, kc, vc, lfc = to_chunks(qq), to_chunks(kk), to_chunks(vv), to_chunks(log_f)
    bc = jnp.cumsum(lfc, axis=3)
    causal = jnp.tril(jnp.ones((CHUNK, CHUNK), dtype=bool))[:, :, None]

    def step(state, xs):
        qt, kt, vt, bt = xs
        rel = bt[:, :, :, None, :] - bt[:, :, None, :, :]
        decay = jnp.where(causal, jnp.exp(jnp.where(causal, rel, 0.0)), 0.0)
        scores = jnp.einsum('bhtd,bhsd,bhtsd->bhts', qt, kt, decay)
        o = (jnp.einsum('bhts,bhsv->bhtv', scores, vt)
             + jnp.einsum('bhtd,bhdv->bhtv', qt * jnp.exp(bt), state))
        b_last = bt[:, :, -1:, :]
        state = (jnp.exp(b_last[:, :, 0, :])[..., None] * state
                 + jnp.einsum('bhsd,bhsv->bhdv', kt * jnp.exp(b_last - bt), vt))
        return state, o

    state0 = jnp.zeros((bsz, HGRN_HEADS, HGRN_DIM, HGRN_DIM), jnp.float32)
    _, o = lax.scan(step, state0, (qc, kc, vc, bc))
    o = o.transpose(1, 0, 3, 2, 4).reshape(shp)
    o = rmsnorm(o, norm_g) * jax.nn.silu(g.astype(jnp.float32).reshape(shp))
    return o.reshape(bsz, s, HGRN_WIDTH).astype(q.dtype)


def conv_glu_ffn(h, w_up, w_gate, conv_w, conv_b, w_down):
    up = h @ w_up
    gate = h @ w_gate
    up = lax.conv_general_dilated(up, conv_w[:, None, :], window_strides=(1,),
                                  padding=[(CONV_WIDTH - 1, 0)],
                                  dimension_numbers=('NWC', 'WIO', 'NWC'),
                                  feature_group_count=D_FF) + conv_b
    return (jax.nn.gelu(up) * gate) @ w_down


def setup_inputs(seed: int = 0) -> dict:
    key = jax.random.key(seed)
    ks = jax.random.split(key, 32)
    f32 = jnp.float32
    L, G, N, P = DEPTH, S5_GROUPS, S5_STATE, S5_GROUP

    def nrm(k, shape, scale):
        return jax.random.normal(k, shape, f32) * scale

    x = nrm(ks[0], (BATCH, SEQ, D_MODEL), 1.0)
    norm_mix_g = 1.0 + nrm(ks[1], (L, D_MODEL), 0.02)
    w_in = nrm(ks[2], (L, D_MODEL, D_IN), D_MODEL ** -0.5)
    s5_lambda_re = jnp.minimum(-0.5 + nrm(ks[3], (L, G, N), 0.02), -0.1)
    s5_lambda_im = (math.pi * jnp.arange(N, dtype=f32))[None, None, :] + nrm(ks[4], (L, G, N), 0.01)
    s5_log_step = jax.random.uniform(ks[5], (L, G), f32, math.log(S5_DT_MIN), math.log(S5_DT_MAX))
    s5_b_re = nrm(ks[6], (L, G, N, P), (2.0 * P) ** -0.5)
    s5_b_im = nrm(ks[7], (L, G, N, P), (2.0 * P) ** -0.5)
    s5_c_re = nrm(ks[8], (L, G, P, N), (2.0 * N) ** -0.5 * 4.0)
    s5_c_im = nrm(ks[9], (L, G, P, N), (2.0 * N) ** -0.5 * 4.0)
    s5_d = nrm(ks[10], (L, S5_WIDTH), 1.0)
    s5_w_glu = nrm(ks[11], (L, S5_WIDTH, S5_WIDTH), S5_WIDTH ** -0.5)
    diff_lambda_q1 = nrm(ks[12], (L, DIFF_QKDIM), 0.1)
    diff_lambda_k1 = nrm(ks[13], (L, DIFF_QKDIM), 0.1)
    diff_lambda_q2 = nrm(ks[14], (L, DIFF_QKDIM), 0.1)
    diff_lambda_k2 = nrm(ks[15], (L, DIFF_QKDIM), 0.1)
    diff_subln_g = 1.0 + nrm(ks[16], (L, DIFF_VDIM), 0.02)
    hgrn_lb_logits = nrm(ks[17], (L, HGRN_WIDTH), 0.5)
    hgrn_norm_g = 1.0 + nrm(ks[18], (L, HGRN_DIM), 0.02)
    w_out = nrm(ks[19], (L, D_MIX, D_MODEL), D_MIX ** -0.5)
    norm_ffn_g = 1.0 + nrm(ks[20], (L, D_MODEL), 0.02)
    w_up = nrm(ks[21], (L, D_MODEL, D_FF), D_MODEL ** -0.5)
    w_gate = nrm(ks[22], (L, D_MODEL, D_FF), D_MODEL ** -0.5)
    conv_w = nrm(ks[23], (L, CONV_WIDTH, D_FF), CONV_WIDTH ** -0.5)
    conv_b = nrm(ks[24], (L, D_FF), 0.02)
    w_down = nrm(ks[25], (L, D_FF, D_MODEL), D_FF ** -0.5)
    final_norm_g = 1.0 + nrm(ks[26], (D_MODEL,), 0.02)
    return {"x": x, "norm_mix_g": norm_mix_g, "w_in": w_in,
            "s5_lambda_re": s5_lambda_re, "s5_lambda_im": s5_lambda_im, "s5_log_step": s5_log_step,
            "s5_b_re": s5_b_re, "s5_b_im": s5_b_im, "s5_c_re": s5_c_re, "s5_c_im": s5_c_im,
            "s5_d": s5_d, "s5_w_glu": s5_w_glu,
            "diff_lambda_q1": diff_lambda_q1, "diff_lambda_k1": diff_lambda_k1,
            "diff_lambda_q2": diff_lambda_q2, "diff_lambda_k2": diff_lambda_k2,
            "diff_subln_g": diff_subln_g, "hgrn_lb_logits": hgrn_lb_logits, "hgrn_norm_g": hgrn_norm_g,
            "w_out": w_out, "norm_ffn_g": norm_ffn_g, "w_up": w_up, "w_gate": w_gate,
            "conv_w": conv_w, "conv_b": conv_b, "w_down": w_down, "final_norm_g": final_norm_g}


def reference(x, norm_mix_g, w_in, s5_lambda_re, s5_lambda_im, s5_log_step,
              s5_b_re, s5_b_im, s5_c_re, s5_c_im, s5_d, s5_w_glu,
              diff_lambda_q1, diff_lambda_k1, diff_lambda_q2, diff_lambda_k2,
              diff_subln_g, hgrn_lb_logits, hgrn_norm_g, w_out, norm_ffn_g,
              w_up, w_gate, conv_w, conv_b, w_down, final_norm_g):
    p_lb = jax.nn.softmax(hgrn_lb_logits.astype(jnp.float32), axis=0)
    lower_bounds = jnp.cumsum(p_lb, axis=0) - p_lb[0:1]
    for l in range(DEPTH):
        h = rmsnorm(x, norm_mix_g[l])
        proj = h @ w_in[l]
        u, dq, dk, dv, cq, cf, ci, cg = split_projection(proj)
        y_a = s5_mixer(u, s5_lambda_re[l], s5_lambda_im[l], s5_log_step[l],
                       s5_b_re[l], s5_b_im[l], s5_c_re[l], s5_c_im[l], s5_d[l], s5_w_glu[l])
        y_b = diff_attention(dq, dk, dv, diff_lambda_q1[l], diff_lambda_k1[l],
                             diff_lambda_q2[l], diff_lambda_k2[l], diff_subln_g[l], l)
        y_c = hgrn2_mixer(cq, cf, ci, cg, lower_bounds[l], hgrn_norm_g[l])
        mix = jnp.concatenate([y_a, y_b, y_c], axis=-1)
        x = x + (mix @ w_out[l]).astype(x.dtype)
        h = rmsnorm(x, norm_ffn_g[l])
        x = x + conv_glu_ffn(h, w_up[l], w_gate[l], conv_w[l], conv_b[l], w_down[l]).astype(x.dtype)
    return rmsnorm(x, final_norm_g)
```

```python
import functools
import math

import jax
import jax.numpy as jnp
import numpy as np
from jax import lax
from jax.experimental import pallas as pl
from jax.experimental.pallas import tpu as pltpu

F32 = jnp.float32
BF16 = jnp.bfloat16

D_MODEL = 1024
DEPTH = 2
CHUNK = 64
EPS = 1e-6
MASK_VALUE = -1e30
MIN_FORGET = 1e-6
S5_WIDTH = 256
DIFF_WIDTH = 384
HGRN_WIDTH = 384
S5_GROUP = 16
S5_GROUPS = 16
S5_STATE = 64
S5_LANES = S5_GROUPS * S5_STATE
DIFF_HEADS = 6
DIFF_VDIM = 64
DIFF_QKDIM = 32
HGRN_HEADS = 6
HGRN_DIM = 64
D_FF = 2816
QKV_WIDTH = 3 * DIFF_WIDTH
HG_WIDTH = 4 * HGRN_WIDTH
D_IN = S5_WIDTH + QKV_WIDTH + HG_WIDTH

LANES = 128
SUBLANES = 8
VMEM_LIMIT_BYTES = 56 * 1024 * 1024

PROJ_ROWS = 512
FFN_ROWS = 256
S5_STEPS = 64
S5_LANE_CHUNK = 512
ATTN_TQ = 128
ATTN_TK = 128
HGRN_ROWS = 256
HGRN_SUB = 8

_NT = (((1,), (1,)), ((), ()))


def _rms(x, g):
    ms = jnp.mean(x * x, axis=-1, keepdims=True)
    return x * lax.rsqrt(ms + EPS) * g


def _gelu_tanh(x):
    c = math.sqrt(2.0 / math.pi)
    return 0.5 * x * (1.0 + jnp.tanh(c * (x + 0.044715 * (x * x * x))))


def _sigmoid(x):
    return 1.0 / (1.0 + jnp.exp(-x))


def _split_dot(x, ones_bd):
    hi = x.astype(BF16)
    lo = (x - hi.astype(F32)).astype(BF16)
    return (jnp.dot(hi, ones_bd, preferred_element_type=F32)
            + jnp.dot(lo, ones_bd, preferred_element_type=F32))


def _head_ones(width, head):
    r = lax.broadcasted_iota(jnp.int32, (width, width), 0) // head
    c = lax.broadcasted_iota(jnp.int32, (width, width), 1) // head
    return jnp.where(r == c, 1.0, 0.0).astype(BF16)


def _proj_kernel(x_ref, g_ref, w_ref, u_ref, qkv_ref, hg_ref):
    h = _rms(x_ref[...], g_ref[...]).astype(BF16)
    u_ref[...] = jnp.dot(h, w_ref[:, 0:S5_WIDTH], preferred_element_type=F32)
    qkv_ref[...] = jnp.dot(h, w_ref[:, S5_WIDTH:S5_WIDTH + QKV_WIDTH],
                           preferred_element_type=F32).astype(BF16)
    hg_ref[...] = jnp.dot(h, w_ref[:, S5_WIDTH + QKV_WIDTH:D_IN], preferred_element_type=F32)


def _proj(x2d, g, w_bf16, bsz, seq):
    tt = min(PROJ_ROWS, seq)
    ns = seq // tt
    return pl.pallas_call(
        _proj_kernel,
        grid=(bsz, ns),
        in_specs=[
            pl.BlockSpec((tt, D_MODEL), lambda b, s: (b * ns + s, 0)),
            pl.BlockSpec((1, D_MODEL), lambda b, s: (0, 0)),
            pl.BlockSpec((D_MODEL, D_IN), lambda b, s: (0, 0)),
        ],
        out_specs=[
            pl.BlockSpec((tt, S5_WIDTH), lambda b, s: (s, b)),
            pl.BlockSpec((tt, QKV_WIDTH), lambda b, s: (b * ns + s, 0)),
            pl.BlockSpec((tt, HG_WIDTH), lambda b, s: (b * ns + s, 0)),
        ],
        out_shape=[
            jax.ShapeDtypeStruct((seq, bsz * S5_WIDTH), F32),
            jax.ShapeDtypeStruct((bsz * seq, QKV_WIDTH), BF16),
            jax.ShapeDtypeStruct((bsz * seq, HG_WIDTH), F32),
        ],
        compiler_params=pltpu.CompilerParams(
            dimension_semantics=("arbitrary", "arbitrary"),
            vmem_limit_bytes=VMEM_LIMIT_BYTES),
        name="proj",
    )(x2d, g, w_bf16)


def _s5_kernel(u_ref, lre_ref, lim_ref, lstep_ref, bre_ref, bim_ref, cre_ref, cim_ref,
               d_ref, wglu_ref, o_ref, bb_ref, cc_ref, lam_ref, x_ref, bu_ref, *, bsz, steps):
    n = S5_LANES

    @pl.when(pl.program_id(0) == 0)
    def _():
        lre = lre_ref[...]
        lim = lim_ref[...]
        dt = jnp.exp(lstep_ref[...])
        mag = jnp.exp(lre * dt)
        ang = lim * dt
        lb_re = mag * jnp.cos(ang)
        lb_im = mag * jnp.sin(ang)
        den = lre * lre + lim * lim
        num_re = lb_re - 1.0
        num_im = lb_im
        coef_re = (num_re * lre + num_im * lim) / den
        coef_im = (num_im * lre - num_re * lim) / den
        b_re = bre_ref[...]
        b_im = bim_ref[...]
        bb_ref[:, 0:n] = (coef_re * b_re - coef_im * b_im).astype(BF16)
        bb_ref[:, n:2 * n] = (coef_re * b_im + coef_im * b_re).astype(BF16)
        cc_ref[0:n, :] = cre_ref[...].astype(BF16)
        cc_ref[n:2 * n, :] = (-cim_ref[...]).astype(BF16)
        lam_ref[0:1, :] = lb_re
        lam_ref[1:2, :] = lb_im
        x_ref[...] = jnp.zeros_like(x_ref)

    u = u_ref[...]
    bu_ref[...] = jnp.dot(u.astype(BF16), bb_ref[...], preferred_element_type=F32)

    for c0 in range(0, n, S5_LANE_CHUNK):
        cw = S5_LANE_CHUNK
        a_re = jnp.broadcast_to(lam_ref[0:1, c0:c0 + cw], (bsz, cw))
        a_im = jnp.broadcast_to(lam_ref[1:2, c0:c0 + cw], (bsz, cw))

        def step(t, carry, c0=c0, cw=cw, a_re=a_re, a_im=a_im):
            x_re, x_im = carry
            r0 = pl.multiple_of(t * bsz, bsz)
            n_re = a_re * x_re - a_im * x_im + bu_ref[pl.ds(r0, bsz), c0:c0 + cw]
            n_im = a_re * x_im + a_im * x_re + bu_ref[pl.ds(r0, bsz), n + c0:n + c0 + cw]
            bu_ref[pl.ds(r0, bsz), c0:c0 + cw] = n_re
            bu_ref[pl.ds(r0, bsz), n + c0:n + c0 + cw] = n_im
            return n_re, n_im

        x_re, x_im = lax.fori_loop(
            0, steps, step, (x_ref[:, c0:c0 + cw], x_ref[:, n + c0:n + c0 + cw]))
        x_ref[:, c0:c0 + cw] = x_re
        x_ref[:, n + c0:n + c0 + cw] = x_im

    y = jnp.dot(bu_ref[...].astype(BF16), cc_ref[...], preferred_element_type=F32) + d_ref[...] * u
    z = _gelu_tanh(y)
    gate = jnp.dot(z.astype(BF16), wglu_ref[...], preferred_element_type=F32)
    o_ref[...] = z * _sigmoid(gate)


def _block_diag(blocks):
    g, r, c = blocks.shape
    out = jnp.zeros((g, r, g, c), blocks.dtype)
    idx = jnp.arange(g)
    out = out.at[idx, :, idx, :].set(blocks)
    return out.reshape(g * r, g * c)


def _s5(u_tm, lam_re, lam_im, log_step, b_re, b_im, c_re, c_im, d, w_glu_bf16, bsz, seq):
    steps = min(S5_STEPS, seq)
    rows = steps * bsz
    n = S5_LANES
    lre = lam_re.reshape(1, n)
    lim = lam_im.reshape(1, n)
    lstep = jnp.repeat(log_step, S5_STATE).reshape(1, n)
    bre_big = _block_diag(jnp.transpose(b_re, (0, 2, 1)))
    bim_big = _block_diag(jnp.transpose(b_im, (0, 2, 1)))
    cre_big = _block_diag(jnp.transpose(c_re, (0, 2, 1)))
    cim_big = _block_diag(jnp.transpose(c_im, (0, 2, 1)))
    full = lambda shape: pl.BlockSpec(shape, lambda i: (0, 0))
    return pl.pallas_call(
        functools.partial(_s5_kernel, bsz=bsz, steps=steps),
        grid=(seq // steps,),
        in_specs=[
            pl.BlockSpec((rows, S5_WIDTH), lambda i: (i, 0)),
            full((1, n)), full((1, n)), full((1, n)),
            full((S5_WIDTH, n)), full((S5_WIDTH, n)),
            full((n, S5_WIDTH)), full((n, S5_WIDTH)),
            full((1, S5_WIDTH)), full((S5_WIDTH, S5_WIDTH)),
        ],
        out_specs=pl.BlockSpec((rows, S5_WIDTH), lambda i: (i, 0)),
        out_shape=jax.ShapeDtypeStruct((seq * bsz, S5_WIDTH), F32),
        scratch_shapes=[
            pltpu.VMEM((S5_WIDTH, 2 * n), BF16),
            pltpu.VMEM((2 * n, S5_WIDTH), BF16),
            pltpu.VMEM((2, n), F32),
            pltpu.VMEM((bsz, 2 * n), F32),
            pltpu.VMEM((rows, 2 * n), F32),
        ],
        compiler_params=pltpu.CompilerParams(
            dimension_semantics=("arbitrary",),
            vmem_limit_bytes=VMEM_LIMIT_BYTES),
        name="s5",
    )(u_tm, lre, lim, lstep, bre_big, bim_big, cre_big, cim_big, d.reshape(1, S5_WIDTH), w_glu_bf16)


def _attn_kernel(q_ref, k_ref, v_ref, lq1_ref, lk1_ref, lq2_ref, lk2_ref, g_ref, o_ref,
                 m_ref, l_ref, acc_ref, *, layer_idx, tq, tk):
    i = pl.program_id(1)
    lam_init = 0.8 - 0.6 * math.exp(-0.3 * layer_idx)
    lam = (jnp.exp(jnp.sum(lq1_ref[...] * lk1_ref[...], axis=-1, keepdims=True))
           - jnp.exp(jnp.sum(lq2_ref[...] * lk2_ref[...], axis=-1, keepdims=True))
           + lam_init)
    scale = DIFF_QKDIM ** -0.5
    n_tiles = DIFF_WIDTH // LANES
    groups = LANES // DIFF_QKDIM

    m_ref[...] = jnp.full(m_ref.shape, -jnp.inf, F32)
    l_ref[...] = jnp.zeros(l_ref.shape, F32)
    acc_ref[...] = jnp.zeros(acc_ref.shape, F32)

    lane = lax.broadcasted_iota(jnp.int32, (tq, LANES), 1)
    lhs = []
    for t in range(n_tiles):
        qt = q_ref[:, t * LANES:(t + 1) * LANES]
        lhs.append(jnp.concatenate(
            [jnp.where(lane // DIFF_QKDIM == g, qt, jnp.zeros_like(qt)) for g in range(groups)],
            axis=0))

    q_pos = i * tq + lax.broadcasted_iota(jnp.int32, (tq, tk), 0)
    col = lax.broadcasted_iota(jnp.int32, (tq, tk), 1)
    n_kv = ((i + 1) * tq + tk - 1) // tk

    def kv_step(j, carry):
        k0 = pl.multiple_of(j * tk, tk)
        k_pos = k0 + col
        visible = (k_pos // CHUNK) <= (q_pos // CHUNK)
        dist = jnp.abs(q_pos - k_pos).astype(F32)
        for t in range(n_tiles):
            kt = k_ref[pl.ds(k0, tk), t * LANES:(t + 1) * LANES]
            vt = v_ref[pl.ds(k0, tk), t * LANES:(t + 1) * LANES]
            s_all = lax.dot_general(lhs[t], kt, _NT, preferred_element_type=F32)
            p_parts = []
            alphas = []
            for g in range(groups):
                head = 2 * t + g // 2
                slope = 2.0 ** (-8.0 * (head + 1) / DIFF_HEADS)
                idx = t * groups + g
                s = s_all[g * tq:(g + 1) * tq] * scale
                s = jnp.where(visible, s - slope * dist, MASK_VALUE)
                m_old = m_ref[idx]
                m_new = jnp.maximum(m_old, jnp.max(s, axis=-1, keepdims=True))
                p = jnp.exp(s - m_new)
                alpha = jnp.exp(m_old - m_new)
                l_ref[idx] = alpha * l_ref[idx] + jnp.sum(p, axis=-1, keepdims=True)
                m_ref[idx] = m_new
                p_parts.append(p.astype(BF16))
                alphas.append(alpha)
            pv = jnp.dot(jnp.concatenate(p_parts, axis=0), vt, preferred_element_type=F32)
            for g in range(groups):
                idx = t * groups + g
                acc_ref[idx] = alphas[g] * acc_ref[idx] + pv[g * tq:(g + 1) * tq]
        return carry

    lax.fori_loop(0, n_kv, kv_step, 0)

    ones_bd = _head_ones(LANES, DIFF_VDIM)
    for t in range(n_tiles):
        outs = []
        for hh in range(2):
            i0 = t * groups + 2 * hh
            o0 = acc_ref[i0] / l_ref[i0]
            o1 = acc_ref[i0 + 1] / l_ref[i0 + 1]
            outs.append(o0 - lam * o1)
        o = jnp.where(lane < DIFF_VDIM, outs[0], outs[1])
        ms = _split_dot(o * o, ones_bd) * (1.0 / DIFF_VDIM)
        o = o * lax.rsqrt(ms + EPS) * g_ref[:, t * LANES:(t + 1) * LANES]
        o_ref[:, t * LANES:(t + 1) * LANES] = o * (1.0 - lam_init)


def _attn(qkv, lq1, lk1, lq2, lk2, subln_g, layer_idx, bsz, seq):
    tq = min(ATTN_TQ, seq)
    tk = min(ATTN_TK, seq)
    nq = seq // tq
    n_groups = DIFF_HEADS * 2
    small = lambda: pl.BlockSpec((1, DIFF_QKDIM), lambda b, i: (0, 0))
    return pl.pallas_call(
        functools.partial(_attn_kernel, layer_idx=layer_idx, tq=tq, tk=tk),
        grid=(bsz, nq),
        in_specs=[
            pl.BlockSpec((tq, DIFF_WIDTH), lambda b, i: (b * nq + i, 0)),
            pl.BlockSpec((seq, DIFF_WIDTH), lambda b, i: (b, 1)),
            pl.BlockSpec((seq, DIFF_WIDTH), lambda b, i: (b, 2)),
            small(), small(), small(), small(),
            pl.BlockSpec((1, DIFF_WIDTH), lambda b, i: (0, 0)),
        ],
        out_specs=pl.BlockSpec((tq, DIFF_WIDTH), lambda b, i: (b * nq + i, 0)),
        out_shape=jax.ShapeDtypeStruct((bsz * seq, DIFF_WIDTH), F32),
        scratch_shapes=[
            pltpu.VMEM((n_groups, tq, 1), F32),
            pltpu.VMEM((n_groups, tq, 1), F32),
            pltpu.VMEM((n_groups, tq, LANES), F32),
        ],
        compiler_params=pltpu.CompilerParams(
            dimension_semantics=("arbitrary", "arbitrary"),
            vmem_limit_bytes=VMEM_LIMIT_BYTES),
        name="attn",
    )(qkv, qkv, qkv, lq1.reshape(1, -1), lk1.reshape(1, -1), lq2.reshape(1, -1), lk2.reshape(1, -1),
      jnp.tile(subln_g, DIFF_HEADS).reshape(1, DIFF_WIDTH))


def _hgrn_kernel(q_ref, f_ref, i_ref, g_ref, lbl_ref, ng_ref, o_ref, st_ref, *, layer_idx, rows):
    w = HGRN_WIDTH
    c = CHUNK

    @pl.when(pl.program_id(1) == 0)
    def _():
        st_ref[...] = jnp.zeros_like(st_ref)

    logits = lbl_ref[...]
    e = jnp.exp(logits - jnp.max(logits, axis=0, keepdims=True))
    p = e / jnp.sum(e, axis=0, keepdims=True)
    lb = jnp.sum(p[0:layer_idx + 1], axis=0, keepdims=True) - p[0:1]

    row = lax.broadcasted_iota(jnp.int32, (c, w), 0)
    lane_w = lax.broadcasted_iota(jnp.int32, (HGRN_SUB, w), 1)
    bd_mask = (lax.broadcasted_iota(jnp.int32, (w, w), 0) // HGRN_DIM
               == lax.broadcasted_iota(jnp.int32, (w, w), 1) // HGRN_DIM)
    causal = (lax.broadcasted_iota(jnp.int32, (c, c), 1)
              <= lax.broadcasted_iota(jnp.int32, (c, c), 0))
    lane128 = lax.broadcasted_iota(jnp.int32, (c, LANES), 1)
    ones_bd = _head_ones(w, HGRN_DIM)
    n_sub = c // HGRN_SUB
    max_fwd = 60.0

    for ci in range(rows // c):
        r0 = ci * c
        q = q_ref[r0:r0 + c, :]
        f = lb + (1.0 - lb) * _sigmoid(f_ref[r0:r0 + c, :])
        log_f = jnp.log(jnp.maximum(f, MIN_FORGET))
        kk = 1.0 - f
        qq = q * _sigmoid(q)
        vv = i_ref[r0:r0 + c, :]
        vv_b = vv.astype(BF16)

        bc = log_f
        sh = 1
        while sh < c:
            bc = bc + jnp.where(row >= sh, pltpu.roll(bc, sh, axis=0), 0.0)
            sh *= 2

        st = st_ref[...]
        o_inter = lax.dot_general((qq * jnp.exp(bc)).astype(BF16), st.astype(BF16), _NT,
                                  preferred_element_type=F32)
        b_last = bc[c - 1:c, :]
        k2 = (kk * jnp.exp(b_last - bc)).astype(BF16)
        upd = jnp.dot(vv.T.astype(BF16), k2, preferred_element_type=F32)
        st_ref[...] = st * jnp.exp(b_last) + jnp.where(bd_mask, upd, 0.0)

        blocks = []
        for si in range(n_sub):
            s0 = si * HGRN_SUB
            ref_row = bc[s0 + HGRN_SUB // 2:s0 + HGRN_SUB // 2 + 1, :]
            q_s = qq[s0:s0 + HGRN_SUB] * jnp.exp(bc[s0:s0 + HGRN_SUB] - ref_row)
            k_s = (kk * jnp.exp(jnp.minimum(ref_row - bc, max_fwd))).astype(BF16)
            lhs = jnp.concatenate(
                [jnp.where(lane_w // HGRN_DIM == h, q_s, 0.0) for h in range(HGRN_HEADS)],
                axis=0).astype(BF16)
            blocks.append(lax.dot_general(lhs, k_s, _NT, preferred_element_type=F32))

        o_tiles = []
        for t in range(w // LANES):
            v_pair = vv_b[:, t * LANES:(t + 1) * LANES]
            res = []
            for hh in range(2):
                h = 2 * t + hh
                a_h = jnp.concatenate(
                    [blocks[si][h * HGRN_SUB:(h + 1) * HGRN_SUB] for si in range(n_sub)], axis=0)
                a_h = jnp.where(causal, a_h, 0.0).astype(BF16)
                res.append(jnp.dot(a_h, v_pair, preferred_element_type=F32))
            o_tiles.append(jnp.where(lane128 < HGRN_DIM, res[0], res[1]))
        o = jnp.concatenate(o_tiles, axis=1) + o_inter

        ms = _split_dot(o * o, ones_bd) * (1.0 / HGRN_DIM)
        o = o * lax.rsqrt(ms + EPS) * ng_ref[...]
        g = g_ref[r0:r0 + c, :]
        o_ref[r0:r0 + c, :] = o * (g * _sigmoid(g))


def _hgrn(hg, lb_logits, norm_g, layer_idx, bsz, seq):
    rows = min(HGRN_ROWS, seq)
    ns = seq // rows
    w = HGRN_WIDTH
    col = lambda j: pl.BlockSpec((rows, w), lambda b, s, j=j: (b * ns + s, j))
    return pl.pallas_call(
        functools.partial(_hgrn_kernel, layer_idx=layer_idx, rows=rows),
        grid=(bsz, ns),
        in_specs=[col(0), col(1), col(2), col(3),
                  pl.BlockSpec((DEPTH, w), lambda b, s: (0, 0)),
                  pl.BlockSpec((1, w), lambda b, s: (0, 0))],
        out_specs=pl.BlockSpec((rows, w), lambda b, s: (b * ns + s, 0)),
        out_shape=jax.ShapeDtypeStruct((bsz * seq, w), F32),
        scratch_shapes=[pltpu.VMEM((w, w), F32)],
        compiler_params=pltpu.CompilerParams(
            dimension_semantics=("arbitrary", "arbitrary"),
            vmem_limit_bytes=VMEM_LIMIT_BYTES),
        name="hgrn",
    )(hg, hg, hg, hg, lb_logits, jnp.tile(norm_g, HGRN_HEADS).reshape(1, w))


def _ffn_kernel(x_ref, ya_ref, yb_ref, yc_ref, wo_ref, g_ref, wup_ref, wgate_ref, cw_ref, cb_ref,
                wdown_ref, fg_ref, o_ref, up_ref, *, rows, final):
    s = pl.program_id(1)

    @pl.when(s == 0)
    def _():
        up_ref[0:SUBLANES, :] = jnp.zeros((SUBLANES, D_FF), F32)

    mix = jnp.concatenate([ya_ref[...], yb_ref[...], yc_ref[...]], axis=-1).astype(BF16)
    x1 = x_ref[...] + jnp.dot(mix, wo_ref[...], preferred_element_type=F32)
    h = _rms(x1, g_ref[...]).astype(BF16)
    up = jnp.dot(h, wup_ref[...], preferred_element_type=F32)
    gate = jnp.dot(h, wgate_ref[...], preferred_element_type=F32)

    up_ref[SUBLANES:SUBLANES + rows, :] = up
    conv = (cw_ref[2:3, :] * up
            + cw_ref[1:2, :] * up_ref[SUBLANES - 1:SUBLANES - 1 + rows, :]
            + cw_ref[0:1, :] * up_ref[SUBLANES - 2:SUBLANES - 2 + rows, :]
            + cb_ref[...])
    up_ref[0:SUBLANES, :] = up[rows - SUBLANES:rows, :]
    act = (_gelu_tanh(conv) * gate).astype(BF16)
    x2 = x1 + jnp.dot(act, wdown_ref[...], preferred_element_type=F32)
    if final:
        x2 = _rms(x2, fg_ref[...])
    o_ref[...] = x2


def _ffn(x2d, ya_tm, yb, yc, w_out, g, w_up, w_gate, conv_w, conv_b, w_down, final_g, final, bsz, seq):
    rows = min(FFN_ROWS, seq)
    ns = seq // rows
    const = lambda shape: pl.BlockSpec(shape, lambda b, s: (0, 0), pipeline_mode=pl.Buffered(1))
    tok = lambda width: pl.BlockSpec((rows, width), lambda b, s: (b * ns + s, 0))
    return pl.pallas_call(
        functools.partial(_ffn_kernel, rows=rows, final=final),
        grid=(bsz, ns),
        in_specs=[
            tok(D_MODEL),
            pl.BlockSpec((rows, S5_WIDTH), lambda b, s: (s, b)),
            tok(DIFF_WIDTH), tok(HGRN_WIDTH),
            const((D_MODEL, D_MODEL)), const((1, D_MODEL)),
            const((D_MODEL, D_FF)), const((D_MODEL, D_FF)),
            const((3, D_FF)), const((1, D_FF)),
            const((D_FF, D_MODEL)), const((1, D_MODEL)),
        ],
        out_specs=tok(D_MODEL),
        out_shape=jax.ShapeDtypeStruct((bsz * seq, D_MODEL), F32),
        scratch_shapes=[pltpu.VMEM((SUBLANES + rows, D_FF), F32)],
        compiler_params=pltpu.CompilerParams(
            dimension_semantics=("arbitrary", "arbitrary"),
            vmem_limit_bytes=VMEM_LIMIT_BYTES),
        name="ffn",
    )(x2d, ya_tm, yb, yc, w_out, g, w_up, w_gate, conv_w, conv_b, w_down, final_g)


def kernel(x, norm_mix_g, w_in, s5_lambda_re, s5_lambda_im, s5_log_step, s5_b_re, s5_b_im, s5_c_re,
           s5_c_im, s5_d, s5_w_glu, diff_lambda_q1, diff_lambda_k1, diff_lambda_q2, diff_lambda_k2,
           diff_subln_g, hgrn_lb_logits, hgrn_norm_g, w_out, norm_ffn_g, w_up, w_gate, conv_w, conv_b,
           w_down, final_norm_g):
    bsz, seq, d = x.shape
    assert d == D_MODEL and seq % CHUNK == 0 and bsz % SUBLANES == 0
    x2d = x.reshape(bsz * seq, d)
    for l in range(DEPTH):
        u_tm, qkv, hg = _proj(x2d, norm_mix_g[l].reshape(1, d), w_in[l].astype(BF16), bsz, seq)
        ya_tm = _s5(u_tm.reshape(seq * bsz, S5_WIDTH), s5_lambda_re[l], s5_lambda_im[l], s5_log_step[l],
                    s5_b_re[l], s5_b_im[l], s5_c_re[l], s5_c_im[l], s5_d[l], s5_w_glu[l].astype(BF16),
                    bsz, seq)
        yb = _attn(qkv, diff_lambda_q1[l], diff_lambda_k1[l], diff_lambda_q2[l], diff_lambda_k2[l],
                   diff_subln_g[l], l, bsz, seq)
        yc = _hgrn(hg, hgrn_lb_logits, hgrn_norm_g[l], l, bsz, seq)
        x2d = _ffn(x2d, ya_tm.reshape(seq, bsz * S5_WIDTH), yb, yc, w_out[l].astype(BF16),
                   norm_ffn_g[l].reshape(1, d), w_up[l].astype(BF16), w_gate[l].astype(BF16),
                   conv_w[l], conv_b[l].reshape(1, D_FF), w_down[l].astype(BF16),
                   final_norm_g.reshape(1, d), l == DEPTH - 1, bsz, seq)
    return x2d.reshape(bsz, seq, d)
```

```python
import functools
import math

import jax
import jax.numpy as jnp
import numpy as np
from jax import lax
from jax.experimental import pallas as pl
from jax.experimental.pallas import tpu as pltpu

F32 = jnp.float32
BF16 = jnp.bfloat16

D_MODEL = 1024
DEPTH = 2
CHUNK = 64
EPS = 1e-6
MASK_VALUE = -1e30
MIN_FORGET = 1e-6
S5_WIDTH = 256
DIFF_WIDTH = 384
HGRN_WIDTH = 384
S5_GROUP = 16
S5_GROUPS = 16
S5_STATE = 64
S5_LANES = S5_GROUPS * S5_STATE
DIFF_HEADS = 6
DIFF_VDIM = 64
DIFF_QKDIM = 32
HGRN_HEADS = 6
HGRN_DIM = 64
D_FF = 2816
QKV_WIDTH = 3 * DIFF_WIDTH
HG_WIDTH = 4 * HGRN_WIDTH
D_IN = S5_WIDTH + QKV_WIDTH + HG_WIDTH

LANES = 128
SUBLANES = 8
VMEM_LIMIT_BYTES = 56 * 1024 * 1024

PROJ_ROWS = 512
FFN_ROWS = 256
S5_STEPS = 64
S5_LANE_CHUNK = 512
ATTN_TQ = 256
HGRN_ROWS = 256
HGRN_SUB = 8

_NT = (((1,), (1,)), ((), ()))
LOG2E = math.log2(math.e)
ATTN_Q_SCALE = DIFF_QKDIM ** -0.5 * LOG2E
ATTN_VT_ROWS = LANES + 16


def _rms(x, g):
    ms = jnp.mean(x * x, axis=-1, keepdims=True)
    return x * lax.rsqrt(ms + EPS) * g


def _gelu_tanh(x):
    c = math.sqrt(2.0 / math.pi)
    return 0.5 * x * (1.0 + jnp.tanh(c * (x + 0.044715 * (x * x * x))))


def _sigmoid(x):
    return 1.0 / (1.0 + jnp.exp(-x))


def _split_dot(x, ones_bd):
    hi = x.astype(BF16)
    lo = (x - hi.astype(F32)).astype(BF16)
    return (jnp.dot(hi, ones_bd, preferred_element_type=F32)
            + jnp.dot(lo, ones_bd, preferred_element_type=F32))


def _head_ones(width, head):
    r = lax.broadcasted_iota(jnp.int32, (width, width), 0) // head
    c = lax.broadcasted_iota(jnp.int32, (width, width), 1) // head
    return jnp.where(r == c, 1.0, 0.0).astype(BF16)


def _proj_kernel(x_ref, g_ref, w_ref, u_ref, qkv_ref, hg_ref):
    h = _rms(x_ref[...], g_ref[...]).astype(BF16)
    u_ref[...] = jnp.dot(h, w_ref[:, 0:S5_WIDTH], preferred_element_type=F32)
    q0 = S5_WIDTH
    k0 = S5_WIDTH + DIFF_WIDTH
    qkv_ref[:, 0:DIFF_WIDTH] = (jnp.dot(h, w_ref[:, q0:k0], preferred_element_type=F32)
                                * ATTN_Q_SCALE).astype(BF16)
    qkv_ref[:, DIFF_WIDTH:QKV_WIDTH] = jnp.dot(h, w_ref[:, k0:q0 + QKV_WIDTH],
                                               preferred_element_type=F32).astype(BF16)
    hg_ref[...] = jnp.dot(h, w_ref[:, S5_WIDTH + QKV_WIDTH:D_IN], preferred_element_type=F32)


def _proj(x2d, g, w_bf16, bsz, seq):
    tt = min(PROJ_ROWS, seq)
    ns = seq // tt
    return pl.pallas_call(
        _proj_kernel,
        grid=(bsz, ns),
        in_specs=[
            pl.BlockSpec((tt, D_MODEL), lambda b, s: (b * ns + s, 0)),
            pl.BlockSpec((1, D_MODEL), lambda b, s: (0, 0)),
            pl.BlockSpec((D_MODEL, D_IN), lambda b, s: (0, 0)),
        ],
        out_specs=[
            pl.BlockSpec((tt, S5_WIDTH), lambda b, s: (s, b)),
            pl.BlockSpec((tt, QKV_WIDTH), lambda b, s: (b * ns + s, 0)),
            pl.BlockSpec((tt, HG_WIDTH), lambda b, s: (b * ns + s, 0)),
        ],
        out_shape=[
            jax.ShapeDtypeStruct((seq, bsz * S5_WIDTH), F32),
            jax.ShapeDtypeStruct((bsz * seq, QKV_WIDTH), BF16),
            jax.ShapeDtypeStruct((bsz * seq, HG_WIDTH), F32),
        ],
        compiler_params=pltpu.CompilerParams(
            dimension_semantics=("arbitrary", "arbitrary"),
            vmem_limit_bytes=VMEM_LIMIT_BYTES),
        name="proj",
    )(x2d, g, w_bf16)


def _s5_kernel(u_ref, lre_ref, lim_ref, lstep_ref, bre_ref, bim_ref, cre_ref, cim_ref,
               d_ref, wglu_ref, o_ref, bb_ref, cc_ref, lam_ref, x_ref, bu_ref, *, bsz, steps):
    n = S5_LANES

    @pl.when(pl.program_id(0) == 0)
    def _():
        lre = lre_ref[...]
        lim = lim_ref[...]
        dt = jnp.exp(lstep_ref[...])
        mag = jnp.exp(lre * dt)
        ang = lim * dt
        lb_re = mag * jnp.cos(ang)
        lb_im = mag * jnp.sin(ang)
        den = lre * lre + lim * lim
        num_re = lb_re - 1.0
        num_im = lb_im
        coef_re = (num_re * lre + num_im * lim) / den
        coef_im = (num_im * lre - num_re * lim) / den
        b_re = bre_ref[...]
        b_im = bim_ref[...]
        bb_ref[:, 0:n] = (coef_re * b_re - coef_im * b_im).astype(BF16)
        bb_ref[:, n:2 * n] = (coef_re * b_im + coef_im * b_re).astype(BF16)
        cc_ref[0:n, :] = cre_ref[...].astype(BF16)
        cc_ref[n:2 * n, :] = (-cim_ref[...]).astype(BF16)
        lam_ref[0:1, :] = lb_re
        lam_ref[1:2, :] = lb_im
        x_ref[...] = jnp.zeros_like(x_ref)

    u = u_ref[...]
    bu_ref[...] = jnp.dot(u.astype(BF16), bb_ref[...], preferred_element_type=F32)

    for c0 in range(0, n, S5_LANE_CHUNK):
        cw = S5_LANE_CHUNK
        a_re = jnp.broadcast_to(lam_ref[0:1, c0:c0 + cw], (bsz, cw))
        a_im = jnp.broadcast_to(lam_ref[1:2, c0:c0 + cw], (bsz, cw))

        def step(t, carry, c0=c0, cw=cw, a_re=a_re, a_im=a_im):
            x_re, x_im = carry
            r0 = pl.multiple_of(t * bsz, bsz)
            n_re = a_re * x_re - a_im * x_im + bu_ref[pl.ds(r0, bsz), c0:c0 + cw]
            n_im = a_re * x_im + a_im * x_re + bu_ref[pl.ds(r0, bsz), n + c0:n + c0 + cw]
            bu_ref[pl.ds(r0, bsz), c0:c0 + cw] = n_re
            bu_ref[pl.ds(r0, bsz), n + c0:n + c0 + cw] = n_im
            return n_re, n_im

        x_re, x_im = lax.fori_loop(
            0, steps, step, (x_ref[:, c0:c0 + cw], x_ref[:, n + c0:n + c0 + cw]))
        x_ref[:, c0:c0 + cw] = x_re
        x_ref[:, n + c0:n + c0 + cw] = x_im

    y = jnp.dot(bu_ref[...].astype(BF16), cc_ref[...], preferred_element_type=F32) + d_ref[...] * u
    z = _gelu_tanh(y)
    gate = jnp.dot(z.astype(BF16), wglu_ref[...], preferred_element_type=F32)
    o_ref[...] = z * _sigmoid(gate)


def _block_diag(blocks):
    g, r, c = blocks.shape
    out = jnp.zeros((g, r, g, c), blocks.dtype)
    idx = jnp.arange(g)
    out = out.at[idx, :, idx, :].set(blocks)
    return out.reshape(g * r, g * c)


def _s5(u_tm, lam_re, lam_im, log_step, b_re, b_im, c_re, c_im, d, w_glu_bf16, bsz, seq):
    steps = min(S5_STEPS, seq)
    rows = steps * bsz
    n = S5_LANES
    lre = lam_re.reshape(1, n)
    lim = lam_im.reshape(1, n)
    lstep = jnp.repeat(log_step, S5_STATE).reshape(1, n)
    bre_big = _block_diag(jnp.transpose(b_re, (0, 2, 1)))
    bim_big = _block_diag(jnp.transpose(b_im, (0, 2, 1)))
    cre_big = _block_diag(jnp.transpose(c_re, (0, 2, 1)))
    cim_big = _block_diag(jnp.transpose(c_im, (0, 2, 1)))
    full = lambda shape: pl.BlockSpec(shape, lambda i: (0, 0))
    return pl.pallas_call(
        functools.partial(_s5_kernel, bsz=bsz, steps=steps),
        grid=(seq // steps,),
        in_specs=[
            pl.BlockSpec((rows, S5_WIDTH), lambda i: (i, 0)),
            full((1, n)), full((1, n)), full((1, n)),
            full((S5_WIDTH, n)), full((S5_WIDTH, n)),
            full((n, S5_WIDTH)), full((n, S5_WIDTH)),
            full((1, S5_WIDTH)), full((S5_WIDTH, S5_WIDTH)),
        ],
        out_specs=pl.BlockSpec((rows, S5_WIDTH), lambda i: (i, 0)),
        out_shape=jax.ShapeDtypeStruct((seq * bsz, S5_WIDTH), F32),
        scratch_shapes=[
            pltpu.VMEM((S5_WIDTH, 2 * n), BF16),
            pltpu.VMEM((2 * n, S5_WIDTH), BF16),
            pltpu.VMEM((2, n), F32),
            pltpu.VMEM((bsz, 2 * n), F32),
            pltpu.VMEM((rows, 2 * n), F32),
        ],
        compiler_params=pltpu.CompilerParams(
            dimension_semantics=("arbitrary",),
            vmem_limit_bytes=VMEM_LIMIT_BYTES),
        name="s5",
    )(u_tm, lre, lim, lstep, bre_big, bim_big, cre_big, cim_big, d.reshape(1, S5_WIDTH), w_glu_bf16)


def _attn_kernel(q_ref, k_ref, v_ref, lq1_ref, lk1_ref, lq2_ref, lk2_ref, g_ref, o_ref,
                 vt_ref, boff_ref, bdiag_ref, m_ref, l_ref, acc_ref, *, layer_idx, tq, seq):
    i = pl.program_id(1)
    tk = tq
    lam_init = 0.8 - 0.6 * math.exp(-0.3 * layer_idx)
    n_tiles = DIFF_WIDTH // LANES
    groups = LANES // DIFF_QKDIM
    slopes = [LOG2E * 2.0 ** (-8.0 * (h + 1) / DIFF_HEADS) for h in range(DIFF_HEADS)]

    @pl.when((pl.program_id(0) == 0) & (i == 0))
    def _():
        r = lax.broadcasted_iota(jnp.int32, (tk, tq), 0)
        c = lax.broadcasted_iota(jnp.int32, (tk, tq), 1)
        rel = (c - r).astype(F32)
        visible = (r // CHUNK) <= (c // CHUNK)
        for h in range(DIFF_HEADS):
            boff_ref[h] = -slopes[h] * rel
            bdiag_ref[h] = jnp.where(visible, -slopes[h] * jnp.abs(rel), MASK_VALUE)

    @pl.when(i == 0)
    def _():
        extra = lax.broadcasted_iota(jnp.int32, (ATTN_VT_ROWS - LANES, tk), 0)
        ones_row = jnp.where(extra == 0, 1.0, 0.0).astype(BF16)
        for j in range(seq // tk):
            v_t = v_ref[j * tk:(j + 1) * tk, :].astype(F32).T.astype(BF16)
            for t in range(n_tiles):
                vt_ref[j, t, 0:LANES, :] = v_t[t * LANES:(t + 1) * LANES]
                vt_ref[j, t, LANES:ATTN_VT_ROWS, :] = ones_row

    lane = lax.broadcasted_iota(jnp.int32, (tq, LANES), 1)
    lhs = []
    for t in range(n_tiles):
        qt = q_ref[:, t * LANES:(t + 1) * LANES]
        lhs.append(jnp.concatenate(
            [jnp.where(lane // DIFF_QKDIM == g, qt, jnp.zeros_like(qt)) for g in range(groups)],
            axis=0))

    def tile_step(j, bias_ref, first):
        k0 = pl.multiple_of(j * tk, tk)
        gap = ((i - j) * tq).astype(F32)
        for t in range(n_tiles):
            kt = k_ref[pl.ds(k0, tk), t * LANES:(t + 1) * LANES]
            s_all = lax.dot_general(kt, lhs[t], _NT, preferred_element_type=F32)
            p_parts = []
            alphas = []
            for g in range(groups):
                head = 2 * t + g // 2
                cols = slice(g * tq, (g + 1) * tq)
                s = s_all[:, cols] + bias_ref[head]
                s_max = jnp.max(s, axis=0, keepdims=True)
                if first:
                    m_new = s_max
                    shift = s_max
                else:
                    off = -slopes[head] * gap
                    m_old = m_ref[t, :, cols]
                    m_new = jnp.maximum(m_old, s_max + off)
                    shift = m_new - off
                    alphas.append(jnp.exp2(m_old - m_new))
                m_ref[t, :, cols] = m_new
                p_parts.append(jnp.exp2(s - shift).astype(BF16))
            pv = jnp.dot(vt_ref[j, t], jnp.concatenate(p_parts, axis=1),
                         preferred_element_type=F32)
            if first:
                acc_ref[t] = pv[0:LANES]
                l_ref[t] = pv[LANES:LANES + 1]
            else:
                alpha = jnp.concatenate(alphas, axis=1)
                acc_ref[t] = alpha * acc_ref[t] + pv[0:LANES]
                l_ref[t] = alpha * l_ref[t] + pv[LANES:LANES + 1]

    tile_step(i, bdiag_ref, True)

    def off_diag(j, carry):
        tile_step(j, boff_ref, False)
        return carry

    lax.fori_loop(0, i, off_diag, 0)

    lam = (jnp.exp(jnp.sum(lq1_ref[...] * lk1_ref[...], axis=-1, keepdims=True))
           - jnp.exp(jnp.sum(lq2_ref[...] * lk2_ref[...], axis=-1, keepdims=True))
           + lam_init)
    for t in range(n_tiles):
        halves = []
        for hh in range(2):
            rows = slice(hh * DIFF_VDIM, (hh + 1) * DIFF_VDIM)
            c0 = slice((2 * hh) * tq, (2 * hh + 1) * tq)
            c1 = slice((2 * hh + 1) * tq, (2 * hh + 2) * tq)
            o0 = acc_ref[t, rows, c0] / l_ref[t, :, c0]
            o1 = acc_ref[t, rows, c1] / l_ref[t, :, c1]
            o = o0 - lam * o1
            ms = jnp.mean(o * o, axis=0, keepdims=True)
            halves.append(o * lax.rsqrt(ms + EPS))
        o_t = jnp.concatenate(halves, axis=0) * g_ref[t * LANES:(t + 1) * LANES, :]
        o_ref[:, t * LANES:(t + 1) * LANES] = o_t.T * (1.0 - lam_init)


def _attn(qkv, lq1, lk1, lq2, lk2, subln_g, layer_idx, bsz, seq):
    tq = min(ATTN_TQ, seq)
    nq = seq // tq
    n_tiles = DIFF_WIDTH // LANES
    small = lambda: pl.BlockSpec((1, DIFF_QKDIM), lambda b, i: (0, 0))
    return pl.pallas_call(
        functools.partial(_attn_kernel, layer_idx=layer_idx, tq=tq, seq=seq),
        grid=(bsz, nq),
        in_specs=[
            pl.BlockSpec((tq, DIFF_WIDTH), lambda b, i: (b * nq + i, 0)),
            pl.BlockSpec((seq, DIFF_WIDTH), lambda b, i: (b, 1)),
            pl.BlockSpec((seq, DIFF_WIDTH), lambda b, i: (b, 2)),
            small(), small(), small(), small(),
            pl.BlockSpec((DIFF_WIDTH, 1), lambda b, i: (0, 0)),
        ],
        out_specs=pl.BlockSpec((tq, DIFF_WIDTH), lambda b, i: (b * nq + i, 0)),
        out_shape=jax.ShapeDtypeStruct((bsz * seq, DIFF_WIDTH), F32),
        scratch_shapes=[
            pltpu.VMEM((seq // tq, n_tiles, ATTN_VT_ROWS, tq), BF16),
            pltpu.VMEM((DIFF_HEADS, tq, tq), F32),
            pltpu.VMEM((DIFF_HEADS, tq, tq), F32),
            pltpu.VMEM((n_tiles, 1, 4 * tq), F32),
            pltpu.VMEM((n_tiles, 1, 4 * tq), F32),
            pltpu.VMEM((n_tiles, LANES, 4 * tq), F32),
        ],
        compiler_params=pltpu.CompilerParams(
            dimension_semantics=("arbitrary", "arbitrary"),
            vmem_limit_bytes=VMEM_LIMIT_BYTES),
        name="attn",
    )(qkv, qkv, qkv, lq1.reshape(1, -1), lk1.reshape(1, -1), lq2.reshape(1, -1), lk2.reshape(1, -1),
      jnp.tile(subln_g, DIFF_HEADS).reshape(DIFF_WIDTH, 1))


def _hgrn_kernel(q_ref, f_ref, i_ref, g_ref, lbl_ref, ng_ref, o_ref, st_ref, *, layer_idx, rows):
    w = HGRN_WIDTH
    c = CHUNK

    @pl.when(pl.program_id(1) == 0)
    def _():
        st_ref[...] = jnp.zeros_like(st_ref)

    logits = lbl_ref[...]
    e = jnp.exp(logits - jnp.max(logits, axis=0, keepdims=True))
    p = e / jnp.sum(e, axis=0, keepdims=True)
    lb = jnp.sum(p[0:layer_idx + 1], axis=0, keepdims=True) - p[0:1]

    row = lax.broadcasted_iota(jnp.int32, (c, w), 0)
    lane_w = lax.broadcasted_iota(jnp.int32, (HGRN_SUB, w), 1)
    bd_mask = (lax.broadcasted_iota(jnp.int32, (w, w), 0) // HGRN_DIM
               == lax.broadcasted_iota(jnp.int32, (w, w), 1) // HGRN_DIM)
    causal = (lax.broadcasted_iota(jnp.int32, (c, c), 1)
              <= lax.broadcasted_iota(jnp.int32, (c, c), 0))
    lane128 = lax.broadcasted_iota(jnp.int32, (c, LANES), 1)
    ones_bd = _head_ones(w, HGRN_DIM)
    n_sub = c // HGRN_SUB
    max_fwd = 60.0

    for ci in range(rows // c):
        r0 = ci * c
        q = q_ref[r0:r0 + c, :]
        f = lb + (1.0 - lb) * _sigmoid(f_ref[r0:r0 + c, :])
        log_f = jnp.log(jnp.maximum(f, MIN_FORGET))
        kk = 1.0 - f
        qq = q * _sigmoid(q)
        vv = i_ref[r0:r0 + c, :]
        vv_b = vv.astype(BF16)

        bc = log_f
        sh = 1
        while sh < c:
            bc = bc + jnp.where(row >= sh, pltpu.roll(bc, sh, axis=0), 0.0)
            sh *= 2

        st = st_ref[...]
        o_inter = lax.dot_general((qq * jnp.exp(bc)).astype(BF16), st.astype(BF16), _NT,
                                  preferred_element_type=F32)
        b_last = bc[c - 1:c, :]
        k2 = (kk * jnp.exp(b_last - bc)).astype(BF16)
        upd = jnp.dot(vv.T.astype(BF16), k2, preferred_element_type=F32)
        st_ref[...] = st * jnp.exp(b_last) + jnp.where(bd_mask, upd, 0.0)

        blocks = []
        for si in range(n_sub):
            s0 = si * HGRN_SUB
            ref_row = bc[s0 + HGRN_SUB // 2:s0 + HGRN_SUB // 2 + 1, :]
            q_s = qq[s0:s0 + HGRN_SUB] * jnp.exp(bc[s0:s0 + HGRN_SUB] - ref_row)
            k_s = (kk * jnp.exp(jnp.minimum(ref_row - bc, max_fwd))).astype(BF16)
            lhs = jnp.concatenate(
                [jnp.where(lane_w // HGRN_DIM == h, q_s, 0.0) for h in range(HGRN_HEADS)],
                axis=0).astype(BF16)
            blocks.append(lax.dot_general(lhs, k_s, _NT, preferred_element_type=F32))

        o_tiles = []
        for t in range(w // LANES):
            v_pair = vv_b[:, t * LANES:(t + 1) * LANES]
            res = []
            for hh in range(2):
                h = 2 * t + hh
                a_h = jnp.concatenate(
                    [blocks[si][h * HGRN_SUB:(h + 1) * HGRN_SUB] for si in range(n_sub)], axis=0)
                a_h = jnp.where(causal, a_h, 0.0).astype(BF16)
                res.append(jnp.dot(a_h, v_pair, preferred_element_type=F32))
            o_tiles.append(jnp.where(lane128 < HGRN_DIM, res[0], res[1]))
        o = jnp.concatenate(o_tiles, axis=1) + o_inter

        ms = _split_dot(o * o, ones_bd) * (1.0 / HGRN_DIM)
        o = o * lax.rsqrt(ms + EPS) * ng_ref[...]
        g = g_ref[r0:r0 + c, :]
        o_ref[r0:r0 + c, :] = o * (g * _sigmoid(g))


def _hgrn(hg, lb_logits, norm_g, layer_idx, bsz, seq):
    rows = min(HGRN_ROWS, seq)
    ns = seq // rows
    w = HGRN_WIDTH
    col = lambda j: pl.BlockSpec((rows, w), lambda b, s, j=j: (b * ns + s, j))
    return pl.pallas_call(
        functools.partial(_hgrn_kernel, layer_idx=layer_idx, rows=rows),
        grid=(bsz, ns),
        in_specs=[col(0), col(1), col(2), col(3),
                  pl.BlockSpec((DEPTH, w), lambda b, s: (0, 0)),
                  pl.BlockSpec((1, w), lambda b, s: (0, 0))],
        out_specs=pl.BlockSpec((rows, w), lambda b, s: (b * ns + s, 0)),
        out_shape=jax.ShapeDtypeStruct((bsz * seq, w), F32),
        scratch_shapes=[pltpu.VMEM((w, w), F32)],
        compiler_params=pltpu.CompilerParams(
            dimension_semantics=("arbitrary", "arbitrary"),
            vmem_limit_bytes=VMEM_LIMIT_BYTES),
        name="hgrn",
    )(hg, hg, hg, hg, lb_logits, jnp.tile(norm_g, HGRN_HEADS).reshape(1, w))


def _ffn_kernel(x_ref, ya_ref, yb_ref, yc_ref, wo_ref, g_ref, wup_ref, wgate_ref, cw_ref, cb_ref,
                wdown_ref, fg_ref, o_ref, up_ref, *, rows, final):
    s = pl.program_id(1)

    @pl.when(s == 0)
    def _():
        up_ref[0:SUBLANES, :] = jnp.zeros((SUBLANES, D_FF), F32)

    mix = jnp.concatenate([ya_ref[...], yb_ref[...], yc_ref[...]], axis=-1).astype(BF16)
    x1 = x_ref[...] + jnp.dot(mix, wo_ref[...], preferred_element_type=F32)
    h = _rms(x1, g_ref[...]).astype(BF16)
    up = jnp.dot(h, wup_ref[...], preferred_element_type=F32)
    gate = jnp.dot(h, wgate_ref[...], preferred_element_type=F32)

    up_ref[SUBLANES:SUBLANES + rows, :] = up
    conv = (cw_ref[2:3, :] * up
            + cw_ref[1:2, :] * up_ref[SUBLANES - 1:SUBLANES - 1 + rows, :]
            + cw_ref[0:1, :] * up_ref[SUBLANES - 2:SUBLANES - 2 + rows, :]
            + cb_ref[...])
    up_ref[0:SUBLANES, :] = up[rows - SUBLANES:rows, :]
    act = (_gelu_tanh(conv) * gate).astype(BF16)
    x2 = x1 + jnp.dot(act, wdown_ref[...], preferred_element_type=F32)
    if final:
        x2 = _rms(x2, fg_ref[...])
    o_ref[...] = x2


def _ffn(x2d, ya_tm, yb, yc, w_out, g, w_up, w_gate, conv_w, conv_b, w_down, final_g, final, bsz, seq):
    rows = min(FFN_ROWS, seq)
    ns = seq // rows
    const = lambda shape: pl.BlockSpec(shape, lambda b, s: (0, 0), pipeline_mode=pl.Buffered(1))
    tok = lambda width: pl.BlockSpec((rows, width), lambda b, s: (b * ns + s, 0))
    return pl.pallas_call(
        functools.partial(_ffn_kernel, rows=rows, final=final),
        grid=(bsz, ns),
        in_specs=[
            tok(D_MODEL),
            pl.BlockSpec((rows, S5_WIDTH), lambda b, s: (s, b)),
            tok(DIFF_WIDTH), tok(HGRN_WIDTH),
            const((D_MODEL, D_MODEL)), const((1, D_MODEL)),
            const((D_MODEL, D_FF)), const((D_MODEL, D_FF)),
            const((3, D_FF)), const((1, D_FF)),
            const((D_FF, D_MODEL)), const((1, D_MODEL)),
        ],
        out_specs=tok(D_MODEL),
        out_shape=jax.ShapeDtypeStruct((bsz * seq, D_MODEL), F32),
        scratch_shapes=[pltpu.VMEM((SUBLANES + rows, D_FF), F32)],
        compiler_params=pltpu.CompilerParams(
            dimension_semantics=("arbitrary", "arbitrary"),
            vmem_limit_bytes=VMEM_LIMIT_BYTES),
        name="ffn",
    )(x2d, ya_tm, yb, yc, w_out, g, w_up, w_gate, conv_w, conv_b, w_down, final_g)


def kernel(x, norm_mix_g, w_in, s5_lambda_re, s5_lambda_im, s5_log_step, s5_b_re, s5_b_im, s5_c_re,
           s5_c_im, s5_d, s5_w_glu, diff_lambda_q1, diff_lambda_k1, diff_lambda_q2, diff_lambda_k2,
           diff_subln_g, hgrn_lb_logits, hgrn_norm_g, w_out, norm_ffn_g, w_up, w_gate, conv_w, conv_b,
           w_down, final_norm_g):
    bsz, seq, d = x.shape
    assert d == D_MODEL and seq % CHUNK == 0 and bsz % SUBLANES == 0
    x2d = x.reshape(bsz * seq, d)
    for l in range(DEPTH):
        u_tm, qkv, hg = _proj(x2d, norm_mix_g[l].reshape(1, d), w_in[l].astype(BF16), bsz, seq)
        ya_tm = _s5(u_tm.reshape(seq * bsz, S5_WIDTH), s5_lambda_re[l], s5_lambda_im[l], s5_log_step[l],
                    s5_b_re[l], s5_b_im[l], s5_c_re[l], s5_c_im[l], s5_d[l], s5_w_glu[l].astype(BF16),
                    bsz, seq)
        yb = _attn(qkv, diff_lambda_q1[l], diff_lambda_k1[l], diff_lambda_q2[l], diff_lambda_k2[l],
                   diff_subln_g[l], l, bsz, seq)
        yc = _hgrn(hg, hgrn_lb_logits, hgrn_norm_g[l], l, bsz, seq)
        x2d = _ffn(x2d, ya_tm.reshape(seq, bsz * S5_WIDTH), yb, yc, w_out[l].astype(BF16),
                   norm_ffn_g[l].reshape(1, d), w_up[l].astype(BF16), w_gate[l].astype(BF16),
                   conv_w[l], conv_b[l].reshape(1, D_FF), w_down[l].astype(BF16),
                   final_norm_g.reshape(1, d), l == DEPTH - 1, bsz, seq)
    return x2d.reshape(bsz, seq, d)
```

```python
import functools
import math

import jax
import jax.numpy as jnp
import numpy as np
from jax import lax
from jax.experimental import pallas as pl
from jax.experimental.pallas import tpu as pltpu

F32 = jnp.float32
BF16 = jnp.bfloat16

D_MODEL = 1024
DEPTH = 2
CHUNK = 64
EPS = 1e-6
MASK_VALUE = -1e30
MIN_FORGET = 1e-6
S5_WIDTH = 256
DIFF_WIDTH = 384
HGRN_WIDTH = 384
S5_GROUP = 16
S5_GROUPS = 16
S5_STATE = 64
S5_LANES = S5_GROUPS * S5_STATE
DIFF_HEADS = 6
DIFF_VDIM = 64
DIFF_QKDIM = 32
HGRN_HEADS = 6
HGRN_DIM = 64
D_FF = 2816
QKV_WIDTH = 3 * DIFF_WIDTH
HG_WIDTH = 4 * HGRN_WIDTH
D_IN = S5_WIDTH + QKV_WIDTH + HG_WIDTH

LANES = 128
SUBLANES = 8
VMEM_LIMIT_BYTES = 56 * 1024 * 1024

PROJ_ROWS = 512
FFN_ROWS = 256
S5_STEPS = 64
S5_LANE_CHUNK = 512
ATTN_TQ = 256
HGRN_ROWS = 256
HGRN_SUB = 8

_NT = (((1,), (1,)), ((), ()))
LOG2E = math.log2(math.e)
ATTN_Q_SCALE = DIFF_QKDIM ** -0.5 * LOG2E
ATTN_VT_ROWS = DIFF_VDIM + 16


def _rms(x, g):
    ms = jnp.mean(x * x, axis=-1, keepdims=True)
    return x * lax.rsqrt(ms + EPS) * g


def _gelu_tanh(x):
    c = math.sqrt(2.0 / math.pi)
    return 0.5 * x * (1.0 + jnp.tanh(c * (x + 0.044715 * (x * x * x))))


def _sigmoid(x):
    return 0.5 * jnp.tanh(0.5 * x) + 0.5


def _split_dot(x, ones_bd):
    hi = x.astype(BF16)
    lo = (x - hi.astype(F32)).astype(BF16)
    return (jnp.dot(hi, ones_bd, preferred_element_type=F32)
            + jnp.dot(lo, ones_bd, preferred_element_type=F32))


def _head_ones(width, head):
    r = lax.broadcasted_iota(jnp.int32, (width, width), 0) // head
    c = lax.broadcasted_iota(jnp.int32, (width, width), 1) // head
    return jnp.where(r == c, 1.0, 0.0).astype(BF16)


def _proj_kernel(x_ref, g_ref, w_ref, u_ref, qkv_ref, hg_ref):
    h = _rms(x_ref[...], g_ref[...]).astype(BF16)
    u_ref[...] = jnp.dot(h, w_ref[:, 0:S5_WIDTH], preferred_element_type=F32)
    q0 = S5_WIDTH
    k0 = S5_WIDTH + DIFF_WIDTH
    qkv_ref[:, 0:DIFF_WIDTH] = (jnp.dot(h, w_ref[:, q0:k0], preferred_element_type=F32)
                                * ATTN_Q_SCALE).astype(BF16)
    qkv_ref[:, DIFF_WIDTH:QKV_WIDTH] = jnp.dot(h, w_ref[:, k0:q0 + QKV_WIDTH],
                                               preferred_element_type=F32).astype(BF16)
    hg_ref[...] = jnp.dot(h, w_ref[:, S5_WIDTH + QKV_WIDTH:D_IN], preferred_element_type=F32)


def _proj(x2d, g, w_bf16, bsz, seq):
    tt = min(PROJ_ROWS, seq)
    ns = seq // tt
    return pl.pallas_call(
        _proj_kernel,
        grid=(bsz, ns),
        in_specs=[
            pl.BlockSpec((tt, D_MODEL), lambda b, s: (b * ns + s, 0)),
            pl.BlockSpec((1, D_MODEL), lambda b, s: (0, 0)),
            pl.BlockSpec((D_MODEL, D_IN), lambda b, s: (0, 0)),
        ],
        out_specs=[
            pl.BlockSpec((tt, S5_WIDTH), lambda b, s: (s, b)),
            pl.BlockSpec((tt, QKV_WIDTH), lambda b, s: (b * ns + s, 0)),
            pl.BlockSpec((tt, HG_WIDTH), lambda b, s: (b * ns + s, 0)),
        ],
        out_shape=[
            jax.ShapeDtypeStruct((seq, bsz * S5_WIDTH), F32),
            jax.ShapeDtypeStruct((bsz * seq, QKV_WIDTH), BF16),
            jax.ShapeDtypeStruct((bsz * seq, HG_WIDTH), F32),
        ],
        compiler_params=pltpu.CompilerParams(
            dimension_semantics=("arbitrary", "arbitrary"),
            vmem_limit_bytes=VMEM_LIMIT_BYTES),
        name="proj",
    )(x2d, g, w_bf16)


def _s5_kernel(u_ref, lre_ref, lim_ref, lstep_ref, bre_ref, bim_ref, cre_ref, cim_ref,
               d_ref, wglu_ref, o_ref, bb_ref, cc_ref, lam_ref, x_ref, bu_ref, *, bsz, steps):
    n = S5_LANES

    @pl.when(pl.program_id(0) == 0)
    def _():
        lre = lre_ref[...]
        lim = lim_ref[...]
        dt = jnp.exp(lstep_ref[...])
        mag = jnp.exp(lre * dt)
        ang = lim * dt
        lb_re = mag * jnp.cos(ang)
        lb_im = mag * jnp.sin(ang)
        den = lre * lre + lim * lim
        num_re = lb_re - 1.0
        num_im = lb_im
        coef_re = (num_re * lre + num_im * lim) / den
        coef_im = (num_im * lre - num_re * lim) / den
        b_re = bre_ref[...]
        b_im = bim_ref[...]
        bb_ref[:, 0:n] = (coef_re * b_re - coef_im * b_im).astype(BF16)
        bb_ref[:, n:2 * n] = (coef_re * b_im + coef_im * b_re).astype(BF16)
        cc_ref[0:n, :] = cre_ref[...].astype(BF16)
        cc_ref[n:2 * n, :] = (-cim_ref[...]).astype(BF16)
        lam_ref[0:1, :] = lb_re
        lam_ref[1:2, :] = lb_im
        x_ref[...] = jnp.zeros_like(x_ref)

    u = u_ref[...]
    bu_ref[...] = jnp.dot(u.astype(BF16), bb_ref[...], preferred_element_type=F32)

    for c0 in range(0, n, S5_LANE_CHUNK):
        cw = S5_LANE_CHUNK
        a_re = jnp.broadcast_to(lam_ref[0:1, c0:c0 + cw], (bsz, cw))
        a_im = jnp.broadcast_to(lam_ref[1:2, c0:c0 + cw], (bsz, cw))

        def step(t, carry, c0=c0, cw=cw, a_re=a_re, a_im=a_im):
            x_re, x_im = carry
            r0 = pl.multiple_of(t * bsz, bsz)
            n_re = a_re * x_re - a_im * x_im + bu_ref[pl.ds(r0, bsz), c0:c0 + cw]
            n_im = a_re * x_im + a_im * x_re + bu_ref[pl.ds(r0, bsz), n + c0:n + c0 + cw]
            bu_ref[pl.ds(r0, bsz), c0:c0 + cw] = n_re
            bu_ref[pl.ds(r0, bsz), n + c0:n + c0 + cw] = n_im
            return n_re, n_im

        x_re, x_im = lax.fori_loop(
            0, steps, step, (x_ref[:, c0:c0 + cw], x_ref[:, n + c0:n + c0 + cw]))
        x_ref[:, c0:c0 + cw] = x_re
        x_ref[:, n + c0:n + c0 + cw] = x_im

    y = jnp.dot(bu_ref[...].astype(BF16), cc_ref[...], preferred_element_type=F32) + d_ref[...] * u
    z = _gelu_tanh(y)
    gate = jnp.dot(z.astype(BF16), wglu_ref[...], preferred_element_type=F32)
    o_ref[...] = z * _sigmoid(gate)


def _block_diag(blocks):
    g, r, c = blocks.shape
    out = jnp.zeros((g, r, g, c), blocks.dtype)
    idx = jnp.arange(g)
    out = out.at[idx, :, idx, :].set(blocks)
    return out.reshape(g * r, g * c)


def _s5(u_tm, lam_re, lam_im, log_step, b_re, b_im, c_re, c_im, d, w_glu_bf16, bsz, seq):
    steps = min(S5_STEPS, seq)
    rows = steps * bsz
    n = S5_LANES
    lre = lam_re.reshape(1, n)
    lim = lam_im.reshape(1, n)
    lstep = jnp.repeat(log_step, S5_STATE).reshape(1, n)
    bre_big = _block_diag(jnp.transpose(b_re, (0, 2, 1)))
    bim_big = _block_diag(jnp.transpose(b_im, (0, 2, 1)))
    cre_big = _block_diag(jnp.transpose(c_re, (0, 2, 1)))
    cim_big = _block_diag(jnp.transpose(c_im, (0, 2, 1)))
    full = lambda shape: pl.BlockSpec(shape, lambda i: (0, 0))
    return pl.pallas_call(
        functools.partial(_s5_kernel, bsz=bsz, steps=steps),
        grid=(seq // steps,),
        in_specs=[
            pl.BlockSpec((rows, S5_WIDTH), lambda i: (i, 0)),
            full((1, n)), full((1, n)), full((1, n)),
            full((S5_WIDTH, n)), full((S5_WIDTH, n)),
            full((n, S5_WIDTH)), full((n, S5_WIDTH)),
            full((1, S5_WIDTH)), full((S5_WIDTH, S5_WIDTH)),
        ],
        out_specs=pl.BlockSpec((rows, S5_WIDTH), lambda i: (i, 0)),
        out_shape=jax.ShapeDtypeStruct((seq * bsz, S5_WIDTH), F32),
        scratch_shapes=[
            pltpu.VMEM((S5_WIDTH, 2 * n), BF16),
            pltpu.VMEM((2 * n, S5_WIDTH), BF16),
            pltpu.VMEM((2, n), F32),
            pltpu.VMEM((bsz, 2 * n), F32),
            pltpu.VMEM((rows, 2 * n), F32),
        ],
        compiler_params=pltpu.CompilerParams(
            dimension_semantics=("arbitrary",),
            vmem_limit_bytes=VMEM_LIMIT_BYTES),
        name="s5",
    )(u_tm, lre, lim, lstep, bre_big, bim_big, cre_big, cim_big, d.reshape(1, S5_WIDTH), w_glu_bf16)


def _attn_kernel(q_ref, k_ref, v_ref, lq1_ref, lk1_ref, lq2_ref, lk2_ref, g_ref, o_ref,
                 vt_ref, boff_ref, bdiag_ref, lhs_ref, m_ref, acc_ref, *, layer_idx, tq, seq):
    i = pl.program_id(1)
    tk = tq
    lam_init = 0.8 - 0.6 * math.exp(-0.3 * layer_idx)
    n_tiles = DIFF_WIDTH // LANES
    groups = LANES // DIFF_QKDIM
    slopes = [LOG2E * 2.0 ** (-8.0 * (h + 1) / DIFF_HEADS) for h in range(DIFF_HEADS)]

    @pl.when((pl.program_id(0) == 0) & (i == 0))
    def _():
        r = lax.broadcasted_iota(jnp.int32, (tk, tq), 0)
        c = lax.broadcasted_iota(jnp.int32, (tk, tq), 1)
        rel = (c - r).astype(F32)
        visible = (r // CHUNK) <= (c // CHUNK)
        for h in range(DIFF_HEADS):
            boff_ref[h] = -slopes[h] * rel
            bdiag_ref[h] = jnp.where(visible, -slopes[h] * jnp.abs(rel), MASK_VALUE)

    @pl.when(i == 0)
    def _():
        extra = lax.broadcasted_iota(jnp.int32, (ATTN_VT_ROWS - DIFF_VDIM, tk), 0)
        ones_row = jnp.where(extra == 0, 1.0, 0.0).astype(BF16)
        for j in range(seq // tk):
            v_t = v_ref[j * tk:(j + 1) * tk, :].astype(F32).T.astype(BF16)
            for h in range(DIFF_HEADS):
                vt_ref[j, h, 0:DIFF_VDIM, :] = v_t[h * DIFF_VDIM:(h + 1) * DIFF_VDIM]
                vt_ref[j, h, DIFF_VDIM:ATTN_VT_ROWS, :] = ones_row

    lane = lax.broadcasted_iota(jnp.int32, (tq, LANES), 1)
    for t in range(n_tiles):
        qt = q_ref[:, t * LANES:(t + 1) * LANES]
        for g in range(groups):
            lhs_ref[t, g * tq:(g + 1) * tq, :] = jnp.where(lane // DIFF_QKDIM == g, qt, jnp.zeros_like(qt))

    def raw_scores(j, t):
        k0 = pl.multiple_of(j * tk, tk)
        kt = k_ref[pl.ds(k0, tk), t * LANES:(t + 1) * LANES]
        return lax.dot_general(kt, lhs_ref[t], _NT, preferred_element_type=F32)

    def pipelined(j, consume):
        raw = raw_scores(j, 0)
        for t in range(n_tiles):
            nxt = raw_scores(j, t + 1) if t + 1 < n_tiles else None
            consume(t, raw)
            raw = nxt

    def tile_step(j, bias_ref, first):
        gap = ((i - j) * tq).astype(F32)

        def consume(t, raw):
            for hh in range(2):
                head = 2 * t + hh
                off = -slopes[head] * gap
                p_parts = []
                alphas = []
                for c in range(2):
                    cols = slice((2 * hh + c) * tq, (2 * hh + c + 1) * tq)
                    mcols = slice(c * tq, (c + 1) * tq)
                    s = raw[:, cols] + bias_ref[head]
                    s_max = jnp.max(s, axis=0, keepdims=True)
                    if first:
                        m_new = s_max
                        shift = s_max
                    else:
                        m_old = m_ref[head, :, mcols]
                        m_new = jnp.maximum(m_old, s_max + off)
                        shift = m_new - off
                        alphas.append(jnp.exp2(m_old - m_new))
                    m_ref[head, :, mcols] = m_new
                    p_parts.append(jnp.exp2(s - shift).astype(BF16))
                pv = jnp.dot(vt_ref[j, head], jnp.concatenate(p_parts, axis=1),
                             preferred_element_type=F32)
                if first:
                    acc_ref[head] = pv
                else:
                    acc_ref[head] = jnp.concatenate(alphas, axis=1) * acc_ref[head] + pv

        pipelined(j, consume)

    def fast_step(j, carry):
        gap = ((i - j) * tq).astype(F32)

        def consume(t, raw):
            for hh in range(2):
                head = 2 * t + hh
                row = -slopes[head] * gap - m_ref[head]
                bias = boff_ref[head]
                p_parts = []
                for c in range(2):
                    cols = slice((2 * hh + c) * tq, (2 * hh + c + 1) * tq)
                    p_parts.append(
                        jnp.exp2((raw[:, cols] + bias) + row[:, c * tq:(c + 1) * tq]).astype(BF16))
                acc_ref[head] += jnp.dot(vt_ref[j, head], jnp.concatenate(p_parts, axis=1),
                                         preferred_element_type=F32)

        pipelined(j, consume)
        return carry

    def robust_step(j, carry):
        tile_step(j, boff_ref, False)
        return carry

    tile_step(i, bdiag_ref, True)
    lax.fori_loop(0, i, fast_step, 0)

    finite = jnp.abs(acc_ref[...]) < jnp.inf
    overflowed = jnp.max(jnp.where(finite, 0.0, 1.0)) > 0.0

    @pl.when(overflowed)
    def _():
        tile_step(i, bdiag_ref, True)
        lax.fori_loop(0, i, robust_step, 0)

    lam = (jnp.exp(jnp.sum(lq1_ref[...] * lk1_ref[...], axis=-1, keepdims=True))
           - jnp.exp(jnp.sum(lq2_ref[...] * lk2_ref[...], axis=-1, keepdims=True))
           + lam_init)
    for t in range(n_tiles):
        halves = []
        for hh in range(2):
            head = 2 * t + hh
            o0 = acc_ref[head, 0:DIFF_VDIM, 0:tq] / acc_ref[head, DIFF_VDIM:DIFF_VDIM + 1, 0:tq]
            o1 = (acc_ref[head, 0:DIFF_VDIM, tq:2 * tq]
                  / acc_ref[head, DIFF_VDIM:DIFF_VDIM + 1, tq:2 * tq])
            o = o0 - lam * o1
            ms = jnp.mean(o * o, axis=0, keepdims=True)
            halves.append(o * lax.rsqrt(ms + EPS))
        o_t = jnp.concatenate(halves, axis=0) * g_ref[t * LANES:(t + 1) * LANES, :]
        o_ref[:, t * LANES:(t + 1) * LANES] = o_t.T * (1.0 - lam_init)


def _attn(qkv, lq1, lk1, lq2, lk2, subln_g, layer_idx, bsz, seq):
    tq = min(ATTN_TQ, seq)
    nq = seq // tq
    small = lambda: pl.BlockSpec((1, DIFF_QKDIM), lambda b, i: (0, 0))
    return pl.pallas_call(
        functools.partial(_attn_kernel, layer_idx=layer_idx, tq=tq, seq=seq),
        grid=(bsz, nq),
        in_specs=[
            pl.BlockSpec((tq, DIFF_WIDTH), lambda b, i: (b * nq + i, 0)),
            pl.BlockSpec((seq, DIFF_WIDTH), lambda b, i: (b, 1)),
            pl.BlockSpec((seq, DIFF_WIDTH), lambda b, i: (b, 2)),
            small(), small(), small(), small(),
            pl.BlockSpec((DIFF_WIDTH, 1), lambda b, i: (0, 0)),
        ],
        out_specs=pl.BlockSpec((tq, DIFF_WIDTH), lambda b, i: (b * nq + i, 0)),
        out_shape=jax.ShapeDtypeStruct((bsz * seq, DIFF_WIDTH), F32),
        scratch_shapes=[
            pltpu.VMEM((seq // tq, DIFF_HEADS, ATTN_VT_ROWS, tq), BF16),
            pltpu.VMEM((DIFF_HEADS, tq, tq), F32),
            pltpu.VMEM((DIFF_HEADS, tq, tq), F32),
            pltpu.VMEM((DIFF_WIDTH // LANES, 4 * tq, LANES), BF16),
            pltpu.VMEM((DIFF_HEADS, 1, 2 * tq), F32),
            pltpu.VMEM((DIFF_HEADS, ATTN_VT_ROWS, 2 * tq), F32),
        ],
        compiler_params=pltpu.CompilerParams(
            dimension_semantics=("arbitrary", "arbitrary"),
            vmem_limit_bytes=VMEM_LIMIT_BYTES),
        name="attn",
    )(qkv, qkv, qkv, lq1.reshape(1, -1), lk1.reshape(1, -1), lq2.reshape(1, -1), lk2.reshape(1, -1),
      jnp.tile(subln_g, DIFF_HEADS).reshape(DIFF_WIDTH, 1))


def _hgrn_kernel(q_ref, f_ref, i_ref, g_ref, lbl_ref, ng_ref, o_ref, st_ref, *, layer_idx, rows):
    w = HGRN_WIDTH
    c = CHUNK

    @pl.when(pl.program_id(1) == 0)
    def _():
        st_ref[...] = jnp.zeros_like(st_ref)

    logits = lbl_ref[...]
    e = jnp.exp(logits - jnp.max(logits, axis=0, keepdims=True))
    p = e / jnp.sum(e, axis=0, keepdims=True)
    lb = jnp.sum(p[0:layer_idx + 1], axis=0, keepdims=True) - p[0:1]

    row = lax.broadcasted_iota(jnp.int32, (c, w), 0)
    lane_w = lax.broadcasted_iota(jnp.int32, (HGRN_SUB, w), 1)
    bd_mask = (lax.broadcasted_iota(jnp.int32, (LANES, LANES), 0) // HGRN_DIM
               == lax.broadcasted_iota(jnp.int32, (LANES, LANES), 1) // HGRN_DIM)
    causal = (lax.broadcasted_iota(jnp.int32, (c, c), 1)
              <= lax.broadcasted_iota(jnp.int32, (c, c), 0))
    lane128 = lax.broadcasted_iota(jnp.int32, (c, LANES), 1)
    ones_bd = _head_ones(w, HGRN_DIM)
    n_sub = c // HGRN_SUB

    for ci in range(rows // c):
        r0 = ci * c
        q = q_ref[r0:r0 + c, :]
        f = lb + (1.0 - lb) * _sigmoid(f_ref[r0:r0 + c, :])
        log_f = jnp.log(jnp.maximum(f, MIN_FORGET))
        kk = 1.0 - f
        qq = q * _sigmoid(q)
        vv = i_ref[r0:r0 + c, :]
        vv_b = vv.astype(BF16)

        bc = log_f
        sh = 1
        while sh < c:
            bc = bc + jnp.where(row >= sh, pltpu.roll(bc, sh, axis=0), 0.0)
            sh *= 2

        q_dec = (qq * jnp.exp(bc)).astype(BF16)
        b_last = bc[c - 1:c, :]
        k2 = (kk * jnp.exp(b_last - bc)).astype(BF16)
        vv_t = vv.T.astype(BF16)
        carry_dec = jnp.exp(b_last)
        o_inter_tiles = []
        for t in range(w // LANES):
            tl = slice(t * LANES, (t + 1) * LANES)
            st = st_ref[t]
            o_inter_tiles.append(lax.dot_general(q_dec[:, tl], st.astype(BF16), _NT,
                                                 preferred_element_type=F32))
            upd = jnp.dot(vv_t[tl], k2[:, tl], preferred_element_type=F32)
            st_ref[t] = st * carry_dec[:, tl] + jnp.where(bd_mask, upd, 0.0)
        o_inter = jnp.concatenate(o_inter_tiles, axis=1)

        blocks = []
        for si in range(n_sub):
            s0 = si * HGRN_SUB
            ref_row = bc[s0 + HGRN_SUB // 2:s0 + HGRN_SUB // 2 + 1, :]
            q_s = qq[s0:s0 + HGRN_SUB] * jnp.exp(bc[s0:s0 + HGRN_SUB] - ref_row)
            n_keys = s0 + HGRN_SUB
            k_s = kk[0:n_keys] * jnp.exp(ref_row - bc[0:n_keys])
            if n_keys < c:
                k_s = jnp.concatenate([k_s, jnp.zeros((c - n_keys, w), F32)], axis=0)
            k_s = k_s.astype(BF16)
            lhs = jnp.concatenate(
                [jnp.where(lane_w // HGRN_DIM == h, q_s, 0.0) for h in range(HGRN_HEADS)],
                axis=0).astype(BF16)
            blocks.append(lax.dot_general(lhs, k_s, _NT, preferred_element_type=F32))

        o_tiles = []
        for t in range(w // LANES):
            v_pair = vv_b[:, t * LANES:(t + 1) * LANES]
            res = []
            for hh in range(2):
                h = 2 * t + hh
                a_h = jnp.concatenate(
                    [blocks[si][h * HGRN_SUB:(h + 1) * HGRN_SUB] for si in range(n_sub)], axis=0)
                a_h = jnp.where(causal, a_h, 0.0).astype(BF16)
                res.append(jnp.dot(a_h, v_pair, preferred_element_type=F32))
            o_tiles.append(jnp.where(lane128 < HGRN_DIM, res[0], res[1]))
        o = jnp.concatenate(o_tiles, axis=1) + o_inter

        ms = _split_dot(o * o, ones_bd) * (1.0 / HGRN_DIM)
        o = o * lax.rsqrt(ms + EPS) * ng_ref[...]
        g = g_ref[r0:r0 + c, :]
        o_ref[r0:r0 + c, :] = o * (g * _sigmoid(g))


def _hgrn(hg, lb_logits, norm_g, layer_idx, bsz, seq):
    rows = min(HGRN_ROWS, seq)
    ns = seq // rows
    w = HGRN_WIDTH
    col = lambda j: pl.BlockSpec((rows, w), lambda b, s, j=j: (b * ns + s, j))
    return pl.pallas_call(
        functools.partial(_hgrn_kernel, layer_idx=layer_idx, rows=rows),
        grid=(bsz, ns),
        in_specs=[col(0), col(1), col(2), col(3),
                  pl.BlockSpec((DEPTH, w), lambda b, s: (0, 0)),
                  pl.BlockSpec((1, w), lambda b, s: (0, 0))],
        out_specs=pl.BlockSpec((rows, w), lambda b, s: (b * ns + s, 0)),
        out_shape=jax.ShapeDtypeStruct((bsz * seq, w), F32),
        scratch_shapes=[pltpu.VMEM((w // LANES, LANES, LANES), F32)],
        compiler_params=pltpu.CompilerParams(
            dimension_semantics=("arbitrary", "arbitrary"),
            vmem_limit_bytes=VMEM_LIMIT_BYTES),
        name="hgrn",
    )(hg, hg, hg, hg, lb_logits, jnp.tile(norm_g, HGRN_HEADS).reshape(1, w))


def _ffn_kernel(x_ref, ya_ref, yb_ref, yc_ref, wo_ref, g_ref, wup_ref, wgate_ref, cw_ref, cb_ref,
                wdown_ref, fg_ref, o_ref, up_ref, *, rows, final):
    s = pl.program_id(1)

    @pl.when(s == 0)
    def _():
        up_ref[0:SUBLANES, :] = jnp.zeros((SUBLANES, D_FF), F32)

    mix = jnp.concatenate([ya_ref[...], yb_ref[...], yc_ref[...]], axis=-1).astype(BF16)
    x1 = x_ref[...] + jnp.dot(mix, wo_ref[...], preferred_element_type=F32)
    h = _rms(x1, g_ref[...]).astype(BF16)
    up = jnp.dot(h, wup_ref[...], preferred_element_type=F32)
    gate = jnp.dot(h, wgate_ref[...], preferred_element_type=F32)

    up_ref[SUBLANES:SUBLANES + rows, :] = up
    conv = (cw_ref[2:3, :] * up
            + cw_ref[1:2, :] * up_ref[SUBLANES - 1:SUBLANES - 1 + rows, :]
            + cw_ref[0:1, :] * up_ref[SUBLANES - 2:SUBLANES - 2 + rows, :]
            + cb_ref[...])
    up_ref[0:SUBLANES, :] = up[rows - SUBLANES:rows, :]
    act = (_gelu_tanh(conv) * gate).astype(BF16)
    x2 = x1 + jnp.dot(act, wdown_ref[...], preferred_element_type=F32)
    if final:
        x2 = _rms(x2, fg_ref[...])
    o_ref[...] = x2


def _ffn(x2d, ya_tm, yb, yc, w_out, g, w_up, w_gate, conv_w, conv_b, w_down, final_g, final, bsz, seq):
    rows = min(FFN_ROWS, seq)
    ns = seq // rows
    const = lambda shape: pl.BlockSpec(shape, lambda b, s: (0, 0), pipeline_mode=pl.Buffered(1))
    tok = lambda width: pl.BlockSpec((rows, width), lambda b, s: (b * ns + s, 0))
    return pl.pallas_call(
        functools.partial(_ffn_kernel, rows=rows, final=final),
        grid=(bsz, ns),
        in_specs=[
            tok(D_MODEL),
            pl.BlockSpec((rows, S5_WIDTH), lambda b, s: (s, b)),
            tok(DIFF_WIDTH), tok(HGRN_WIDTH),
            const((D_MODEL, D_MODEL)), const((1, D_MODEL)),
            const((D_MODEL, D_FF)), const((D_MODEL, D_FF)),
            const((3, D_FF)), const((1, D_FF)),
            const((D_FF, D_MODEL)), const((1, D_MODEL)),
        ],
        out_specs=tok(D_MODEL),
        out_shape=jax.ShapeDtypeStruct((bsz * seq, D_MODEL), F32),
        scratch_shapes=[pltpu.VMEM((SUBLANES + rows, D_FF), F32)],
        compiler_params=pltpu.CompilerParams(
            dimension_semantics=("arbitrary", "arbitrary"),
            vmem_limit_bytes=VMEM_LIMIT_BYTES),
        name="ffn",
    )(x2d, ya_tm, yb, yc, w_out, g, w_up, w_gate, conv_w, conv_b, w_down, final_g)


def kernel(x, norm_mix_g, w_in, s5_lambda_re, s5_lambda_im, s5_log_step, s5_b_re, s5_b_im, s5_c_re,
           s5_c_im, s5_d, s5_w_glu, diff_lambda_q1, diff_lambda_k1, diff_lambda_q2, diff_lambda_k2,
           diff_subln_g, hgrn_lb_logits, hgrn_norm_g, w_out, norm_ffn_g, w_up, w_gate, conv_w, conv_b,
           w_down, final_norm_g):
    bsz, seq, d = x.shape
    assert d == D_MODEL and seq % CHUNK == 0 and bsz % SUBLANES == 0
    x2d = x.reshape(bsz * seq, d)
    for l in range(DEPTH):
        u_tm, qkv, hg = _proj(x2d, norm_mix_g[l].reshape(1, d), w_in[l].astype(BF16), bsz, seq)
        ya_tm = _s5(u_tm.reshape(seq * bsz, S5_WIDTH), s5_lambda_re[l], s5_lambda_im[l], s5_log_step[l],
                    s5_b_re[l], s5_b_im[l], s5_c_re[l], s5_c_im[l], s5_d[l], s5_w_glu[l].astype(BF16),
                    bsz, seq)
        yb = _attn(qkv, diff_lambda_q1[l], diff_lambda_k1[l], diff_lambda_q2[l], diff_lambda_k2[l],
                   diff_subln_g[l], l, bsz, seq)
        yc = _hgrn(hg, hgrn_lb_logits, hgrn_norm_g[l], l, bsz, seq)
        x2d = _ffn(x2d, ya_tm.reshape(seq, bsz * S5_WIDTH), yb, yc, w_out[l].astype(BF16),
                   norm_ffn_g[l].reshape(1, d), w_up[l].astype(BF16), w_gate[l].astype(BF16),
                   conv_w[l], conv_b[l].reshape(1, D_FF), w_down[l].astype(BF16),
                   final_norm_g.reshape(1, d), l == DEPTH - 1, bsz, seq)
    return x2d.reshape(bsz, seq, d)
```

```python
import functools
import math

import jax
import jax.numpy as jnp
import numpy as np
from jax import lax
from jax.experimental import pallas as pl
from jax.experimental.pallas import tpu as pltpu

F32 = jnp.float32
BF16 = jnp.bfloat16

D_MODEL = 1024
DEPTH = 2
CHUNK = 64
EPS = 1e-6
MASK_VALUE = -1e30
MIN_FORGET = 1e-6
S5_WIDTH = 256
DIFF_WIDTH = 384
HGRN_WIDTH = 384
S5_GROUP = 16
S5_GROUPS = 16
S5_STATE = 64
S5_LANES = S5_GROUPS * S5_STATE
DIFF_HEADS = 6
DIFF_VDIM = 64
DIFF_QKDIM = 32
HGRN_HEADS = 6
HGRN_DIM = 64
D_FF = 2816
QKV_WIDTH = 3 * DIFF_WIDTH
HG_WIDTH = 4 * HGRN_WIDTH
D_IN = S5_WIDTH + QKV_WIDTH + HG_WIDTH

LANES = 128
SUBLANES = 8
VMEM_LIMIT_BYTES = 56 * 1024 * 1024

PROJ_ROWS = 512
FFN_ROWS = 256
S5_STEPS = 64
S5_LANE_CHUNK = 512
ATTN_TQ = 256
HGRN_ROWS = 256
HGRN_SUB = 8

_NT = (((1,), (1,)), ((), ()))
LOG2E = math.log2(math.e)
ATTN_Q_SCALE = DIFF_QKDIM ** -0.5 * LOG2E
ATTN_SAFE_SUM = 2.0 ** 120
ATTN_VT_ROWS = DIFF_VDIM + 16


def _rms(x, g):
    ms = jnp.mean(x * x, axis=-1, keepdims=True)
    return x * lax.rsqrt(ms + EPS) * g


def _gelu_tanh(x):
    c = math.sqrt(2.0 / math.pi)
    return 0.5 * x * (1.0 + jnp.tanh(c * (x + 0.044715 * (x * x * x))))


def _sigmoid(x):
    return 0.5 * jnp.tanh(0.5 * x) + 0.5


def _split_dot(x, ones_bd):
    hi = x.astype(BF16)
    lo = (x - hi.astype(F32)).astype(BF16)
    return (jnp.dot(hi, ones_bd, preferred_element_type=F32)
            + jnp.dot(lo, ones_bd, preferred_element_type=F32))


def _head_ones(width, head):
    r = lax.broadcasted_iota(jnp.int32, (width, width), 0) // head
    c = lax.broadcasted_iota(jnp.int32, (width, width), 1) // head
    return jnp.where(r == c, 1.0, 0.0).astype(BF16)


def _proj_kernel(x_ref, g_ref, w_ref, u_ref, qkv_ref, hg_ref):
    h = _rms(x_ref[...], g_ref[...]).astype(BF16)
    u_ref[...] = jnp.dot(h, w_ref[:, 0:S5_WIDTH], preferred_element_type=F32).astype(BF16)
    q0 = S5_WIDTH
    k0 = S5_WIDTH + DIFF_WIDTH
    qkv_ref[:, 0:DIFF_WIDTH] = (jnp.dot(h, w_ref[:, q0:k0], preferred_element_type=F32)
                                * ATTN_Q_SCALE).astype(BF16)
    qkv_ref[:, DIFF_WIDTH:QKV_WIDTH] = jnp.dot(h, w_ref[:, k0:q0 + QKV_WIDTH],
                                               preferred_element_type=F32).astype(BF16)
    hg_ref[...] = jnp.dot(h, w_ref[:, S5_WIDTH + QKV_WIDTH:D_IN], preferred_element_type=F32)


def _proj(x2d, g, w_bf16, bsz, seq):
    tt = min(PROJ_ROWS, seq)
    ns = seq // tt
    return pl.pallas_call(
        _proj_kernel,
        grid=(bsz, ns),
        in_specs=[
            pl.BlockSpec((tt, D_MODEL), lambda b, s: (b * ns + s, 0)),
            pl.BlockSpec((1, D_MODEL), lambda b, s: (0, 0)),
            pl.BlockSpec((D_MODEL, D_IN), lambda b, s: (0, 0)),
        ],
        out_specs=[
            pl.BlockSpec((tt, S5_WIDTH), lambda b, s: (s, b)),
            pl.BlockSpec((tt, QKV_WIDTH), lambda b, s: (b * ns + s, 0)),
            pl.BlockSpec((tt, HG_WIDTH), lambda b, s: (b * ns + s, 0)),
        ],
        out_shape=[
            jax.ShapeDtypeStruct((seq, bsz * S5_WIDTH), BF16),
            jax.ShapeDtypeStruct((bsz * seq, QKV_WIDTH), BF16),
            jax.ShapeDtypeStruct((bsz * seq, HG_WIDTH), F32),
        ],
        compiler_params=pltpu.CompilerParams(
            dimension_semantics=("arbitrary", "arbitrary"),
            vmem_limit_bytes=VMEM_LIMIT_BYTES),
        name="proj",
    )(x2d, g, w_bf16)


def _s5_kernel(u_ref, lre_ref, lim_ref, lstep_ref, bre_ref, bim_ref, cre_ref, cim_ref,
               d_ref, wglu_ref, o_ref, bb_ref, cc_ref, lam_ref, x_ref, bu_ref, *, bsz, steps):
    n = S5_LANES

    @pl.when(pl.program_id(0) == 0)
    def _():
        lre = lre_ref[...]
        lim = lim_ref[...]
        dt = jnp.exp(lstep_ref[...])
        mag = jnp.exp(lre * dt)
        ang = lim * dt
        lb_re = mag * jnp.cos(ang)
        lb_im = mag * jnp.sin(ang)
        den = lre * lre + lim * lim
        num_re = lb_re - 1.0
        num_im = lb_im
        coef_re = (num_re * lre + num_im * lim) / den
        coef_im = (num_im * lre - num_re * lim) / den
        b_re = bre_ref[...]
        b_im = bim_ref[...]
        bb_ref[:, 0:n] = (coef_re * b_re - coef_im * b_im).astype(BF16)
        bb_ref[:, n:2 * n] = (coef_re * b_im + coef_im * b_re).astype(BF16)
        cc_ref[0:n, :] = cre_ref[...].astype(BF16)
        cc_ref[n:2 * n, :] = (-cim_ref[...]).astype(BF16)
        lam_ref[0:1, :] = lb_re
        lam_ref[1:2, :] = lb_im
        x_ref[...] = jnp.zeros_like(x_ref)

    u_b = u_ref[...]
    u = u_b.astype(F32)
    half =(steps // 2) * bsz
    bu_ref[0:half, :] = jnp.dot(u_b[0:half], bb_ref[...], preferred_element_type=F32)
    bu_ref[half:, :] = jnp.dot(u_b[half:], bb_ref[...], preferred_element_type=F32)

    for c0 in range(0, n, S5_LANE_CHUNK):
        cw = S5_LANE_CHUNK
        a_re = jnp.broadcast_to(lam_ref[0:1, c0:c0 + cw], (bsz, cw))
        a_im = jnp.broadcast_to(lam_ref[1:2, c0:c0 + cw], (bsz, cw))

        def step(t, carry, c0=c0, cw=cw, a_re=a_re, a_im=a_im):
            x_re, x_im = carry
            r0 = pl.multiple_of(t * bsz, bsz)
            n_re = a_re * x_re - a_im * x_im + bu_ref[pl.ds(r0, bsz), c0:c0 + cw]
            n_im = a_re * x_im + a_im * x_re + bu_ref[pl.ds(r0, bsz), n + c0:n + c0 + cw]
            bu_ref[pl.ds(r0, bsz), c0:c0 + cw] = n_re
            bu_ref[pl.ds(r0, bsz), n + c0:n + c0 + cw] = n_im
            return n_re, n_im

        x_re, x_im = lax.fori_loop(
            0, steps, step, (x_ref[:, c0:c0 + cw], x_ref[:, n + c0:n + c0 + cw]))
        x_ref[:, c0:c0 + cw] = x_re
        x_ref[:, n + c0:n + c0 + cw] = x_im

    y = jnp.concatenate(
        [jnp.dot(bu_ref[0:half, :].astype(BF16), cc_ref[...], preferred_element_type=F32),
         jnp.dot(bu_ref[half:, :].astype(BF16), cc_ref[...], preferred_element_type=F32)],
        axis=0) + d_ref[...] * u
    z = _gelu_tanh(y)
    gate = jnp.dot(z.astype(BF16), wglu_ref[...], preferred_element_type=F32)
    o_ref[...] = (z * _sigmoid(gate)).astype(BF16)


def _block_diag(blocks):
    g, r, c = blocks.shape
    out = jnp.zeros((g, r, g, c), blocks.dtype)
    idx = jnp.arange(g)
    out = out.at[idx, :, idx, :].set(blocks)
    return out.reshape(g * r, g * c)


def _s5(u_tm, lam_re, lam_im, log_step, b_re, b_im, c_re, c_im, d, w_glu_bf16, bsz, seq):
    steps = min(S5_STEPS, seq)
    rows = steps * bsz
    n = S5_LANES
    lre = lam_re.reshape(1, n)
    lim = lam_im.reshape(1, n)
    lstep = jnp.repeat(log_step, S5_STATE).reshape(1, n)
    bre_big = _block_diag(jnp.transpose(b_re, (0, 2, 1)))
    bim_big = _block_diag(jnp.transpose(b_im, (0, 2, 1)))
    cre_big = _block_diag(jnp.transpose(c_re, (0, 2, 1)))
    cim_big = _block_diag(jnp.transpose(c_im, (0, 2, 1)))
    full = lambda shape: pl.BlockSpec(shape, lambda i: (0, 0))
    return pl.pallas_call(
        functools.partial(_s5_kernel, bsz=bsz, steps=steps),
        grid=(seq // steps,),
        in_specs=[
            pl.BlockSpec((rows, S5_WIDTH), lambda i: (i, 0)),
            full((1, n)), full((1, n)), full((1, n)),
            full((S5_WIDTH, n)), full((S5_WIDTH, n)),
            full((n, S5_WIDTH)), full((n, S5_WIDTH)),
            full((1, S5_WIDTH)), full((S5_WIDTH, S5_WIDTH)),
        ],
        out_specs=pl.BlockSpec((rows, S5_WIDTH), lambda i: (i, 0)),
        out_shape=jax.ShapeDtypeStruct((seq * bsz, S5_WIDTH), BF16),
        scratch_shapes=[
            pltpu.VMEM((S5_WIDTH, 2 * n), BF16),
            pltpu.VMEM((2 * n, S5_WIDTH), BF16),
            pltpu.VMEM((2, n), F32),
            pltpu.VMEM((bsz, 2 * n), F32),
            pltpu.VMEM((rows, 2 * n), F32),
        ],
        compiler_params=pltpu.CompilerParams(
            dimension_semantics=("arbitrary",),
            vmem_limit_bytes=VMEM_LIMIT_BYTES),
        name="s5",
    )(u_tm, lre, lim, lstep, bre_big, bim_big, cre_big, cim_big, d.reshape(1, S5_WIDTH), w_glu_bf16)


def _attn_kernel(q_ref, k_ref, v_ref, lq1_ref, lk1_ref, lq2_ref, lk2_ref, g_ref, o_ref,
                 vt_ref, vmax_ref, boff_ref, bdiag_ref, lhs_ref, m_ref, acc_ref, *, layer_idx, tq, seq):
    i = pl.program_id(1)
    tk = tq
    lam_init = 0.8 - 0.6 * math.exp(-0.3 * layer_idx)
    n_tiles = DIFF_WIDTH // LANES
    groups = LANES // DIFF_QKDIM
    slopes = [LOG2E * 2.0 ** (-8.0 * (h + 1) / DIFF_HEADS) for h in range(DIFF_HEADS)]

    @pl.when((pl.program_id(0) == 0) & (i == 0))
    def _():
        r = lax.broadcasted_iota(jnp.int32, (tk, tq), 0)
        c = lax.broadcasted_iota(jnp.int32, (tk, tq), 1)
        rel = (c - r).astype(F32)
        visible = (r // CHUNK) <= (c // CHUNK)
        for h in range(DIFF_HEADS):
            boff_ref[h] = -slopes[h] * rel
            bdiag_ref[h] = jnp.where(visible, -slopes[h] * jnp.abs(rel), MASK_VALUE)

    @pl.when(i == 0)
    def _():
        extra = lax.broadcasted_iota(jnp.int32, (ATTN_VT_ROWS - DIFF_VDIM, tk), 0)
        ones_row = jnp.where(extra == 0, 1.0, 0.0).astype(BF16)
        vmax_ref[...] = jnp.max(jnp.abs(v_ref[...].astype(F32)), axis=(0, 1), keepdims=True)
        for j in range(seq // tk):
            v_t = v_ref[j * tk:(j + 1) * tk, :].astype(F32).T.astype(BF16)
            for h in range(DIFF_HEADS):
                vt_ref[j, h, 0:DIFF_VDIM, :] = v_t[h * DIFF_VDIM:(h + 1) * DIFF_VDIM]
                vt_ref[j, h, DIFF_VDIM:ATTN_VT_ROWS, :] = ones_row

    lane = lax.broadcasted_iota(jnp.int32, (tq, LANES), 1)
    for t in range(n_tiles):
        qt = q_ref[:, t * LANES:(t + 1) * LANES]
        for g in range(groups):
            lhs_ref[t, g * tq:(g + 1) * tq, :] = jnp.where(lane // DIFF_QKDIM == g, qt, jnp.zeros_like(qt))

    def raw_scores(j, t):
        k0 = pl.multiple_of(j * tk, tk)
        kt = k_ref[pl.ds(k0, tk), t * LANES:(t + 1) * LANES]
        return lax.dot_general(kt, lhs_ref[t], _NT, preferred_element_type=F32)

    def pipelined(j, consume):
        raw = raw_scores(j, 0)
        for t in range(n_tiles):
            nxt = raw_scores(j, t + 1) if t + 1 < n_tiles else None
            consume(t, raw)
            raw = nxt

    def tile_step(j, bias_ref, first):
        gap = ((i - j) * tq).astype(F32)

        def consume(t, raw):
            for hh in range(2):
                head = 2 * t + hh
                off = -slopes[head] * gap
                p_parts = []
                alphas = []
                for c in range(2):
                    cols = slice((2 * hh + c) * tq, (2 * hh + c + 1) * tq)
                    mcols = slice(c * tq, (c + 1) * tq)
                    s = raw[:, cols] + bias_ref[head]
                    s_max = jnp.max(s, axis=0, keepdims=True)
                    if first:
                        m_new = s_max
                        shift = s_max
                    else:
                        m_old = m_ref[head, :, mcols]
                        m_new = jnp.maximum(m_old, s_max + off)
                        shift = m_new - off
                        alphas.append(jnp.exp2(m_old - m_new))
                    m_ref[head, :, mcols] = m_new
                    p_parts.append(jnp.exp2(s - shift).astype(BF16))
                pv = jnp.dot(vt_ref[j, head], jnp.concatenate(p_parts, axis=1),
                             preferred_element_type=F32)
                if first:
                    acc_ref[head] = pv
                else:
                    acc_ref[head] = jnp.concatenate(alphas, axis=1) * acc_ref[head] + pv

        pipelined(j, consume)

    def fast_step(j, carry):
        gap = ((i - j) * tq).astype(F32)

        def consume(t, raw):
            for hh in range(2):
                head = 2 * t + hh
                row = -slopes[head] * gap - m_ref[head]
                bias = boff_ref[head]
                p_parts = []
                for c in range(2):
                    cols = slice((2 * hh + c) * tq, (2 * hh + c + 1) * tq)
                    p_parts.append(
                        jnp.exp2((raw[:, cols] + bias) + row[:, c * tq:(c + 1) * tq]).astype(BF16))
                acc_ref[head] += jnp.dot(vt_ref[j, head], jnp.concatenate(p_parts, axis=1),
                                         preferred_element_type=F32)

        pipelined(j, consume)
        return carry

    def robust_step(j, carry):
        tile_step(j, boff_ref, False)
        return carry

    tile_step(i, bdiag_ref, True)
    lax.fori_loop(0, i, fast_step, 0)

    l_max = acc_ref[0, DIFF_VDIM:DIFF_VDIM + 1, :]
    for h in range(1, DIFF_HEADS):
        l_max = jnp.maximum(l_max, acc_ref[h, DIFF_VDIM:DIFF_VDIM + 1, :])
    bound = jnp.max(l_max, axis=1, keepdims=True) * vmax_ref[...]
    overflowed = jnp.logical_not(bound[0, 0] < ATTN_SAFE_SUM)

    @pl.when(overflowed)
    def _():
        tile_step(i, bdiag_ref, True)
        lax.fori_loop(0, i, robust_step, 0)

    lam = (jnp.exp(jnp.sum(lq1_ref[...] * lk1_ref[...], axis=-1, keepdims=True))
           - jnp.exp(jnp.sum(lq2_ref[...] * lk2_ref[...], axis=-1, keepdims=True))
           + lam_init)
    for t in range(n_tiles):
        halves = []
        for hh in range(2):
            head = 2 * t + hh
            inv = 1.0 / acc_ref[head, DIFF_VDIM:DIFF_VDIM + 1, :]
            o0 = acc_ref[head, 0:DIFF_VDIM, 0:tq] * inv[:, 0:tq]
            o1 = acc_ref[head, 0:DIFF_VDIM, tq:2 * tq] * (lam * inv[:, tq:2 * tq])
            o = o0 - o1
            ms = jnp.mean(o * o, axis=0, keepdims=True)
            halves.append(o * lax.rsqrt(ms + EPS))
        o_t = jnp.concatenate(halves, axis=0) * g_ref[t * LANES:(t + 1) * LANES, :]
        o_ref[:, t * LANES:(t + 1) * LANES] = (o_t.T * (1.0 - lam_init)).astype(BF16)


def _attn(qkv, lq1, lk1, lq2, lk2, subln_g, layer_idx, bsz, seq):
    tq = min(ATTN_TQ, seq)
    nq = seq // tq
    small = lambda: pl.BlockSpec((1, DIFF_QKDIM), lambda b, i: (0, 0))
    return pl.pallas_call(
        functools.partial(_attn_kernel, layer_idx=layer_idx, tq=tq, seq=seq),
        grid=(bsz, nq),
        in_specs=[
            pl.BlockSpec((tq, DIFF_WIDTH), lambda b, i: (b * nq + i, 0)),
            pl.BlockSpec((seq, DIFF_WIDTH), lambda b, i: (b, 1)),
            pl.BlockSpec((seq, DIFF_WIDTH), lambda b, i: (b, 2)),
            small(), small(), small(), small(),
            pl.BlockSpec((DIFF_WIDTH, 1), lambda b, i: (0, 0)),
        ],
        out_specs=pl.BlockSpec((tq, DIFF_WIDTH), lambda b, i: (b * nq + i, 0)),
        out_shape=jax.ShapeDtypeStruct((bsz * seq, DIFF_WIDTH), BF16),
        scratch_shapes=[
            pltpu.VMEM((seq // tq, DIFF_HEADS, ATTN_VT_ROWS, tq), BF16),
            pltpu.VMEM((1, 1), F32),
            pltpu.VMEM((DIFF_HEADS, tq, tq), F32),
            pltpu.VMEM((DIFF_HEADS, tq, tq), F32),
            pltpu.VMEM((DIFF_WIDTH // LANES, 4 * tq, LANES), BF16),
            pltpu.VMEM((DIFF_HEADS, 1, 2 * tq), F32),
            pltpu.VMEM((DIFF_HEADS, ATTN_VT_ROWS, 2 * tq), F32),
        ],
        compiler_params=pltpu.CompilerParams(
            dimension_semantics=("arbitrary", "arbitrary"),
            vmem_limit_bytes=VMEM_LIMIT_BYTES),
        name="attn",
    )(qkv, qkv, qkv, lq1.reshape(1, -1), lk1.reshape(1, -1), lq2.reshape(1, -1), lk2.reshape(1, -1),
      jnp.tile(subln_g, DIFF_HEADS).reshape(DIFF_WIDTH, 1))


def _hgrn_kernel(q_ref, f_ref, i_ref, g_ref, lbl_ref, ng_ref, o_ref, st_ref, *, layer_idx, rows):
    w = HGRN_WIDTH
    c = CHUNK

    @pl.when(pl.program_id(1) == 0)
    def _():
        st_ref[...] = jnp.zeros_like(st_ref)

    logits = lbl_ref[...]
    e = jnp.exp(logits - jnp.max(logits, axis=0, keepdims=True))
    p = e / jnp.sum(e, axis=0, keepdims=True)
    lb = jnp.sum(p[0:layer_idx + 1], axis=0, keepdims=True) - p[0:1]

    row = lax.broadcasted_iota(jnp.int32, (c, w), 0)
    lane_w = lax.broadcasted_iota(jnp.int32, (HGRN_SUB, w), 1)
    bd_mask = (lax.broadcasted_iota(jnp.int32, (LANES, LANES), 0) // HGRN_DIM
               == lax.broadcasted_iota(jnp.int32, (LANES, LANES), 1) // HGRN_DIM)
    causal = (lax.broadcasted_iota(jnp.int32, (c, c), 1)
              <= lax.broadcasted_iota(jnp.int32, (c, c), 0))
    lane128 = lax.broadcasted_iota(jnp.int32, (c, LANES), 1)
    ones_bd = _head_ones(w, HGRN_DIM)
    n_sub = c // HGRN_SUB

    for ci in range(rows // c):
        r0 = ci * c
        q = q_ref[r0:r0 + c, :]
        f = lb + (1.0 - lb) * _sigmoid(f_ref[r0:r0 + c, :])
        log_f = jnp.log(jnp.maximum(f, MIN_FORGET))
        kk = 1.0 - f
        qq = q * _sigmoid(q)
        vv = i_ref[r0:r0 + c, :]
        vv_b = vv.astype(BF16)

        bc = log_f
        sh = 1
        while sh < c:
            bc = bc + jnp.where(row >= sh, pltpu.roll(bc, sh, axis=0), 0.0)
            sh *= 2

        q_dec = (qq * jnp.exp(bc)).astype(BF16)
        b_last = bc[c - 1:c, :]
        k2 = (kk * jnp.exp(b_last - bc)).astype(BF16)
        vv_t = vv.T.astype(BF16)
        carry_dec = jnp.exp(b_last)
        o_inter_tiles = []
        for t in range(w // LANES):
            tl = slice(t * LANES, (t + 1) * LANES)
            st = st_ref[t]
            o_inter_tiles.append(lax.dot_general(q_dec[:, tl], st.astype(BF16), _NT,
                                                 preferred_element_type=F32))
            upd = jnp.dot(vv_t[tl], k2[:, tl], preferred_element_type=F32)
            st_ref[t] = st * carry_dec[:, tl] + jnp.where(bd_mask, upd, 0.0)
        o_inter = jnp.concatenate(o_inter_tiles, axis=1)

        blocks = []
        for si in range(n_sub):
            s0 = si * HGRN_SUB
            ref_row = bc[s0 + HGRN_SUB // 2:s0 + HGRN_SUB // 2 + 1, :]
            q_s = qq[s0:s0 + HGRN_SUB] * jnp.exp(bc[s0:s0 + HGRN_SUB] - ref_row)
            n_keys = s0 + HGRN_SUB
            k_s = kk[0:n_keys] * jnp.exp(ref_row - bc[0:n_keys])
            if n_keys < c:
                k_s = jnp.concatenate([k_s, jnp.zeros((c - n_keys, w), F32)], axis=0)
            k_s = k_s.astype(BF16)
            lhs = jnp.concatenate(
                [jnp.where(lane_w // HGRN_DIM == h, q_s, 0.0) for h in range(HGRN_HEADS)],
                axis=0).astype(BF16)
            blocks.append(lax.dot_general(lhs, k_s, _NT, preferred_element_type=F32))

        o_tiles = []
        for t in range(w // LANES):
            v_pair = vv_b[:, t * LANES:(t + 1) * LANES]
            res = []
            for hh in range(2):
                h = 2 * t + hh
                a_h = jnp.concatenate(
                    [blocks[si][h * HGRN_SUB:(h + 1) * HGRN_SUB] for si in range(n_sub)], axis=0)
                a_h = jnp.where(causal, a_h, 0.0).astype(BF16)
                res.append(jnp.dot(a_h, v_pair, preferred_element_type=F32))
            o_tiles.append(jnp.where(lane128 < HGRN_DIM, res[0], res[1]))
        o = jnp.concatenate(o_tiles, axis=1) + o_inter

        ms = _split_dot(o * o, ones_bd) * (1.0 / HGRN_DIM)
        o = o * lax.rsqrt(ms + EPS) * ng_ref[...]
        g = g_ref[r0:r0 + c, :]
        o_ref[r0:r0 + c, :] = (o * (g * _sigmoid(g))).astype(BF16)


def _hgrn(hg, lb_logits, norm_g, layer_idx, bsz, seq):
    rows = min(HGRN_ROWS, seq)
    ns = seq // rows
    w = HGRN_WIDTH
    col = lambda j: pl.BlockSpec((rows, w), lambda b, s, j=j: (b * ns + s, j))
    return pl.pallas_call(
        functools.partial(_hgrn_kernel, layer_idx=layer_idx, rows=rows),
        grid=(bsz, ns),
        in_specs=[col(0), col(1), col(2), col(3),
                  pl.BlockSpec((DEPTH, w), lambda b, s: (0, 0)),
                  pl.BlockSpec((1, w), lambda b, s: (0, 0))],
        out_specs=pl.BlockSpec((rows, w), lambda b, s: (b * ns + s, 0)),
        out_shape=jax.ShapeDtypeStruct((bsz * seq, w), BF16),
        scratch_shapes=[pltpu.VMEM((w // LANES, LANES, LANES), F32)],
        compiler_params=pltpu.CompilerParams(
            dimension_semantics=("arbitrary", "arbitrary"),
            vmem_limit_bytes=VMEM_LIMIT_BYTES),
        name="hgrn",
    )(hg, hg, hg, hg, lb_logits, jnp.tile(norm_g, HGRN_HEADS).reshape(1, w))


def _ffn_kernel(x_ref, ya_ref, yb_ref, yc_ref, wo_ref, g_ref, wup_ref, wgate_ref, cw_ref, cb_ref,
                wdown_ref, fg_ref, o_ref, up_ref, *, rows, final):
    s = pl.program_id(1)

    @pl.when(s == 0)
    def _():
        up_ref[0:SUBLANES, :] = jnp.zeros((SUBLANES, D_FF), F32)

    mix = jnp.concatenate([ya_ref[...], yb_ref[...], yc_ref[...]], axis=-1).astype(BF16)
    x1 = x_ref[...] + jnp.dot(mix, wo_ref[...], preferred_element_type=F32)
    h = _rms(x1, g_ref[...]).astype(BF16)
    up = jnp.dot(h, wup_ref[...], preferred_element_type=F32)
    gate = jnp.dot(h, wgate_ref[...], preferred_element_type=F32)

    up_ref[SUBLANES:SUBLANES + rows, :] = up
    conv = (cw_ref[2:3, :] * up
            + cw_ref[1:2, :] * up_ref[SUBLANES - 1:SUBLANES - 1 + rows, :]
            + cw_ref[0:1, :] * up_ref[SUBLANES - 2:SUBLANES - 2 + rows, :]
            + cb_ref[...])
    up_ref[0:SUBLANES, :] = up[rows - SUBLANES:rows, :]
    act = (_gelu_tanh(conv) * gate).astype(BF16)
    x2 = x1 + jnp.dot(act, wdown_ref[...], preferred_element_type=F32)
    if final:
        x2 = _rms(x2, fg_ref[...])
    o_ref[...] = x2


def _ffn(x2d, ya_tm, yb, yc, w_out, g, w_up, w_gate, conv_w, conv_b, w_down, final_g, final, bsz, seq):
    rows = min(FFN_ROWS, seq)
    ns = seq // rows
    const = lambda shape: pl.BlockSpec(shape, lambda b, s: (0, 0), pipeline_mode=pl.Buffered(1))
    tok = lambda width: pl.BlockSpec((rows, width), lambda b, s: (b * ns + s, 0))
    return pl.pallas_call(
        functools.partial(_ffn_kernel, rows=rows, final=final),
        grid=(bsz, ns),
        in_specs=[
            tok(D_MODEL),
            pl.BlockSpec((rows, S5_WIDTH), lambda b, s: (s, b)),
            tok(DIFF_WIDTH), tok(HGRN_WIDTH),
            const((D_MODEL, D_MODEL)), const((1, D_MODEL)),
            const((D_MODEL, D_FF)), const((D_MODEL, D_FF)),
            const((3, D_FF)), const((1, D_FF)),
            const((D_FF, D_MODEL)), const((1, D_MODEL)),
        ],
        out_specs=tok(D_MODEL),
        out_shape=jax.ShapeDtypeStruct((bsz * seq, D_MODEL), F32),
        scratch_shapes=[pltpu.VMEM((SUBLANES + rows, D_FF), F32)],
        compiler_params=pltpu.CompilerParams(
            dimension_semantics=("arbitrary", "arbitrary"),
            vmem_limit_bytes=VMEM_LIMIT_BYTES),
        name="ffn",
    )(x2d, ya_tm, yb, yc, w_out, g, w_up, w_gate, conv_w, conv_b, w_down, final_g)


def kernel(x, norm_mix_g, w_in, s5_lambda_re, s5_lambda_im, s5_log_step, s5_b_re, s5_b_im, s5_c_re,
           s5_c_im, s5_d, s5_w_glu, diff_lambda_q1, diff_lambda_k1, diff_lambda_q2, diff_lambda_k2,
           diff_subln_g, hgrn_lb_logits, hgrn_norm_g, w_out, norm_ffn_g, w_up, w_gate, conv_w, conv_b,
           w_down, final_norm_g):
    bsz, seq, d = x.shape
    assert d == D_MODEL and seq % CHUNK == 0 and bsz % SUBLANES == 0
    x2d = x.reshape(bsz * seq, d)
    for l in range(DEPTH):
        u_tm, qkv, hg = _proj(x2d, norm_mix_g[l].reshape(1, d), w_in[l].astype(BF16), bsz, seq)
        ya_tm = _s5(u_tm.reshape(seq * bsz, S5_WIDTH), s5_lambda_re[l], s5_lambda_im[l], s5_log_step[l],
                    s5_b_re[l], s5_b_im[l], s5_c_re[l], s5_c_im[l], s5_d[l], s5_w_glu[l].astype(BF16),
                    bsz, seq)
        yb = _attn(qkv, diff_lambda_q1[l], diff_lambda_k1[l], diff_lambda_q2[l], diff_lambda_k2[l],
                   diff_subln_g[l], l, bsz, seq)
        yc = _hgrn(hg, hgrn_lb_logits, hgrn_norm_g[l], l, bsz, seq)
        x2d = _ffn(x2d, ya_tm.reshape(seq, bsz * S5_WIDTH), yb, yc, w_out[l].astype(BF16),
                   norm_ffn_g[l].reshape(1, d), w_up[l].astype(BF16), w_gate[l].astype(BF16),
                   conv_w[l], conv_b[l].reshape(1, D_FF), w_down[l].astype(BF16),
                   final_norm_g.reshape(1, d), l == DEPTH - 1, bsz, seq)
    return x2d.reshape(bsz, seq, d)
```

```python
import functools
import math

import jax
import jax.numpy as jnp
import numpy as np
from jax import lax
from jax.experimental import pallas as pl
from jax.experimental.pallas import tpu as pltpu

F32 = jnp.float32
BF16 = jnp.bfloat16

D_MODEL = 1024
DEPTH = 2
CHUNK = 64
EPS = 1e-6
MASK_VALUE = -1e30
MIN_FORGET = 1e-6
S5_WIDTH = 256
DIFF_WIDTH = 384
HGRN_WIDTH = 384
S5_GROUP = 16
S5_GROUPS = 16
S5_STATE = 64
S5_LANES = S5_GROUPS * S5_STATE
DIFF_HEADS = 6
DIFF_VDIM = 64
DIFF_QKDIM = 32
HGRN_HEADS = 6
HGRN_DIM = 64
D_FF = 2816
QKV_WIDTH = 3 * DIFF_WIDTH
HG_WIDTH = 4 * HGRN_WIDTH
D_IN = S5_WIDTH + QKV_WIDTH + HG_WIDTH

LANES = 128
SUBLANES = 8
VMEM_LIMIT_BYTES = 56 * 1024 * 1024

PROJ_ROWS = 512
FFN_ROWS = 256
S5_STEPS = 64
S5_LANE_CHUNK = 512
ATTN_TQ = 256
HGRN_ROWS = 512
HGRN_SUB = 8

_NT = (((1,), (1,)), ((), ()))
LOG2E = math.log2(math.e)
ATTN_Q_SCALE = DIFF_QKDIM ** -0.5 * LOG2E
ATTN_SAFE_SUM = 2.0 ** 120
ATTN_VT_ROWS = DIFF_VDIM + 16


def _rms(x, g):
    ms = jnp.mean(x * x, axis=-1, keepdims=True)
    return x * lax.rsqrt(ms + EPS) * g


def _gelu_tanh(x):
    c = math.sqrt(2.0 / math.pi)
    return 0.5 * x * (1.0 + jnp.tanh(c * (x + 0.044715 * (x * x * x))))


def _sigmoid(x):
    return 0.5 * jnp.tanh(0.5 * x) + 0.5


def _split_dot(x, ones_bd):
    hi = x.astype(BF16)
    lo = (x - hi.astype(F32)).astype(BF16)
    return (jnp.dot(hi, ones_bd, preferred_element_type=F32)
            + jnp.dot(lo, ones_bd, preferred_element_type=F32))


def _head_ones(width, head):
    r = lax.broadcasted_iota(jnp.int32, (width, width), 0) // head
    c = lax.broadcasted_iota(jnp.int32, (width, width), 1) // head
    return jnp.where(r == c, 1.0, 0.0).astype(BF16)


def _hgrn_lower_bound(logits, layer_idx):
    e = jnp.exp(logits - jnp.max(logits, axis=0, keepdims=True))
    p = e / jnp.sum(e, axis=0, keepdims=True)
    return jnp.sum(p[0:layer_idx + 1], axis=0, keepdims=True) - p[0:1]


def _proj_kernel(x_ref, g_ref, w_ref, lbl_ref, u_ref, qkv_ref, hb_ref, hf_ref, *, layer_idx):
    h = _rms(x_ref[...], g_ref[...]).astype(BF16)
    u_ref[...] = jnp.dot(h, w_ref[:, 0:S5_WIDTH], preferred_element_type=F32).astype(BF16)
    q0 = S5_WIDTH
    k0 = S5_WIDTH + DIFF_WIDTH
    qkv_ref[:, 0:DIFF_WIDTH] = (jnp.dot(h, w_ref[:, q0:k0], preferred_element_type=F32)
                                * ATTN_Q_SCALE).astype(BF16)
    qkv_ref[:, DIFF_WIDTH:QKV_WIDTH] = jnp.dot(h, w_ref[:, k0:q0 + QKV_WIDTH],
                                               preferred_element_type=F32).astype(BF16)
    w = HGRN_WIDTH
    hg = jnp.dot(h, w_ref[:, S5_WIDTH + QKV_WIDTH:D_IN], preferred_element_type=F32)
    lb = _hgrn_lower_bound(lbl_ref[...], layer_idx)
    q = hg[:, 0:w]
    f = lb + (1.0 - lb) * _sigmoid(hg[:, w:2 * w])
    g = hg[:, 3 * w:4 * w]
    hb_ref[:, 0:w] = (q * _sigmoid(q)).astype(BF16)
    hb_ref[:, w:2 * w] = hg[:, 2 * w:3 * w].astype(BF16)
    hb_ref[:, 2 * w:3 * w] = (g * _sigmoid(g)).astype(BF16)
    hf_ref[:, 0:w] = jnp.log(jnp.maximum(f, MIN_FORGET)) * LOG2E
    hf_ref[:, w:2 * w] = 1.0 - f


def _proj(x2d, g, w_bf16, lb_logits, layer_idx, bsz, seq):
    tt = min(PROJ_ROWS, seq)
    ns = seq // tt
    tok = lambda width: pl.BlockSpec((tt, width), lambda b, s: (b * ns + s, 0))
    return pl.pallas_call(
        functools.partial(_proj_kernel, layer_idx=layer_idx),
        grid=(bsz, ns),
        in_specs=[
            tok(D_MODEL),
            pl.BlockSpec((1, D_MODEL), lambda b, s: (0, 0)),
            pl.BlockSpec((D_MODEL, D_IN), lambda b, s: (0, 0)),
            pl.BlockSpec((DEPTH, HGRN_WIDTH), lambda b, s: (0, 0)),
        ],
        out_specs=[
            pl.BlockSpec((tt, S5_WIDTH), lambda b, s: (s, b)),
            tok(QKV_WIDTH), tok(3 * HGRN_WIDTH), tok(2 * HGRN_WIDTH),
        ],
        out_shape=[
            jax.ShapeDtypeStruct((seq, bsz * S5_WIDTH), BF16),
            jax.ShapeDtypeStruct((bsz * seq, QKV_WIDTH), BF16),
            jax.ShapeDtypeStruct((bsz * seq, 3 * HGRN_WIDTH), BF16),
            jax.ShapeDtypeStruct((bsz * seq, 2 * HGRN_WIDTH), F32),
        ],
        compiler_params=pltpu.CompilerParams(
            dimension_semantics=("arbitrary", "arbitrary"),
            vmem_limit_bytes=VMEM_LIMIT_BYTES),
        name="proj",
    )(x2d, g, w_bf16, lb_logits)


def _s5_kernel(u_ref, lre_ref, lim_ref, lstep_ref, bre_ref, bim_ref, cre_ref, cim_ref,
               d_ref, wglu_ref, o_ref, bb_ref, cc_ref, lam_ref, x_ref, bu_ref, *, bsz, steps):
    n = S5_LANES

    @pl.when(pl.program_id(0) == 0)
    def _():
        lre = lre_ref[...]
        lim = lim_ref[...]
        dt = jnp.exp(lstep_ref[...])
        mag = jnp.exp(lre * dt)
        ang = lim * dt
        lb_re = mag * jnp.cos(ang)
        lb_im = mag * jnp.sin(ang)
        den = lre * lre + lim * lim
        num_re = lb_re - 1.0
        num_im = lb_im
        coef_re = (num_re * lre + num_im * lim) / den
        coef_im = (num_im * lre - num_re * lim) / den
        b_re = bre_ref[...]
        b_im = bim_ref[...]
        bb_ref[:, 0:n] = (coef_re * b_re - coef_im * b_im).astype(BF16)
        bb_ref[:, n:2 * n] = (coef_re * b_im + coef_im * b_re).astype(BF16)
        cc_ref[0:n, :] = cre_ref[...].astype(BF16)
        cc_ref[n:2 * n, :] = (-cim_ref[...]).astype(BF16)
        lam_ref[0:1, :] = lb_re
        lam_ref[1:2, :] = lb_im
        x_ref[...] = jnp.zeros_like(x_ref)

    u_b = u_ref[...]
    u = u_b.astype(F32)
    half =(steps // 2) * bsz
    bu_ref[0:half, :] = jnp.dot(u_b[0:half], bb_ref[...], preferred_element_type=F32)
    bu_ref[half:, :] = jnp.dot(u_b[half:], bb_ref[...], preferred_element_type=F32)

    for c0 in range(0, n, S5_LANE_CHUNK):
        cw = S5_LANE_CHUNK
        a_re = jnp.broadcast_to(lam_ref[0:1, c0:c0 + cw], (bsz, cw))
        a_im = jnp.broadcast_to(lam_ref[1:2, c0:c0 + cw], (bsz, cw))

        def step(t, carry, c0=c0, cw=cw, a_re=a_re, a_im=a_im):
            x_re, x_im = carry
            r0 = pl.multiple_of(t * bsz, bsz)
            n_re = a_re * x_re - a_im * x_im + bu_ref[pl.ds(r0, bsz), c0:c0 + cw]
            n_im = a_re * x_im + a_im * x_re + bu_ref[pl.ds(r0, bsz), n + c0:n + c0 + cw]
            bu_ref[pl.ds(r0, bsz), c0:c0 + cw] = n_re
            bu_ref[pl.ds(r0, bsz), n + c0:n + c0 + cw] = n_im
            return n_re, n_im

        x_re, x_im = lax.fori_loop(
            0, steps, step, (x_ref[:, c0:c0 + cw], x_ref[:, n + c0:n + c0 + cw]))
        x_ref[:, c0:c0 + cw] = x_re
        x_ref[:, n + c0:n + c0 + cw] = x_im

    y = jnp.concatenate(
        [jnp.dot(bu_ref[0:half, :].astype(BF16), cc_ref[...], preferred_element_type=F32),
         jnp.dot(bu_ref[half:, :].astype(BF16), cc_ref[...], preferred_element_type=F32)],
        axis=0) + d_ref[...] * u
    z = _gelu_tanh(y)
    gate = jnp.dot(z.astype(BF16), wglu_ref[...], preferred_element_type=F32)
    o_ref[...] = (z * _sigmoid(gate)).astype(BF16)


def _block_diag(blocks):
    g, r, c = blocks.shape
    out = jnp.zeros((g, r, g, c), blocks.dtype)
    idx = jnp.arange(g)
    out = out.at[idx, :, idx, :].set(blocks)
    return out.reshape(g * r, g * c)


def _s5(u_tm, lam_re, lam_im, log_step, b_re, b_im, c_re, c_im, d, w_glu_bf16, bsz, seq):
    steps = min(S5_STEPS, seq)
    rows = steps * bsz
    n = S5_LANES
    lre = lam_re.reshape(1, n)
    lim = lam_im.reshape(1, n)
    lstep = jnp.repeat(log_step, S5_STATE).reshape(1, n)
    bre_big = _block_diag(jnp.transpose(b_re, (0, 2, 1)))
    bim_big = _block_diag(jnp.transpose(b_im, (0, 2, 1)))
    cre_big = _block_diag(jnp.transpose(c_re, (0, 2, 1)))
    cim_big = _block_diag(jnp.transpose(c_im, (0, 2, 1)))
    full = lambda shape: pl.BlockSpec(shape, lambda i: (0, 0))
    return pl.pallas_call(
        functools.partial(_s5_kernel, bsz=bsz, steps=steps),
        grid=(seq // steps,),
        in_specs=[
            pl.BlockSpec((rows, S5_WIDTH), lambda i: (i, 0)),
            full((1, n)), full((1, n)), full((1, n)),
            full((S5_WIDTH, n)), full((S5_WIDTH, n)),
            full((n, S5_WIDTH)), full((n, S5_WIDTH)),
            full((1, S5_WIDTH)), full((S5_WIDTH, S5_WIDTH)),
        ],
        out_specs=pl.BlockSpec((rows, S5_WIDTH), lambda i: (i, 0)),
        out_shape=jax.ShapeDtypeStruct((seq * bsz, S5_WIDTH), BF16),
        scratch_shapes=[
            pltpu.VMEM((S5_WIDTH, 2 * n), BF16),
            pltpu.VMEM((2 * n, S5_WIDTH), BF16),
            pltpu.VMEM((2, n), F32),
            pltpu.VMEM((bsz, 2 * n), F32),
            pltpu.VMEM((rows, 2 * n), F32),
        ],
        compiler_params=pltpu.CompilerParams(
            dimension_semantics=("arbitrary",),
            vmem_limit_bytes=VMEM_LIMIT_BYTES),
        name="s5",
    )(u_tm, lre, lim, lstep, bre_big, bim_big, cre_big, cim_big, d.reshape(1, S5_WIDTH), w_glu_bf16)


def _attn_kernel(q_ref, k_ref, v_ref, lq1_ref, lk1_ref, lq2_ref, lk2_ref, g_ref, o_ref,
                 vt_ref, vmax_ref, boff_ref, bdiag_ref, lhs_ref, m_ref, acc_ref, *, layer_idx, tq, seq):
    i = pl.program_id(1)
    tk = tq
    lam_init = 0.8 - 0.6 * math.exp(-0.3 * layer_idx)
    n_tiles = DIFF_WIDTH // LANES
    groups = LANES // DIFF_QKDIM
    slopes = [LOG2E * 2.0 ** (-8.0 * (h + 1) / DIFF_HEADS) for h in range(DIFF_HEADS)]

    @pl.when((pl.program_id(0) == 0) & (i == 0))
    def _():
        r = lax.broadcasted_iota(jnp.int32, (tk, tq), 0)
        c = lax.broadcasted_iota(jnp.int32, (tk, tq), 1)
        rel = (c - r).astype(F32)
        visible = (r // CHUNK) <= (c // CHUNK)
        for h in range(DIFF_HEADS):
            boff_ref[h] = -slopes[h] * rel
            bdiag_ref[h] = jnp.where(visible, -slopes[h] * jnp.abs(rel), MASK_VALUE)

    @pl.when(i == 0)
    def _():
        extra = lax.broadcasted_iota(jnp.int32, (ATTN_VT_ROWS - DIFF_VDIM, tk), 0)
        ones_row = jnp.where(extra == 0, 1.0, 0.0).astype(BF16)
        vmax_ref[...] = jnp.max(jnp.abs(v_ref[...].astype(F32)), axis=(0, 1), keepdims=True)
        for j in range(seq // tk):
            v_t = v_ref[j * tk:(j + 1) * tk, :].astype(F32).T.astype(BF16)
            for h in range(DIFF_HEADS):
                vt_ref[j, h, 0:DIFF_VDIM, :] = v_t[h * DIFF_VDIM:(h + 1) * DIFF_VDIM]
                vt_ref[j, h, DIFF_VDIM:ATTN_VT_ROWS, :] = ones_row

    lane = lax.broadcasted_iota(jnp.int32, (tq, LANES), 1)
    for t in range(n_tiles):
        qt = q_ref[:, t * LANES:(t + 1) * LANES]
        for g in range(groups):
            lhs_ref[t, g * tq:(g + 1) * tq, :] = jnp.where(lane // DIFF_QKDIM == g, qt, jnp.zeros_like(qt))

    def raw_scores(j, t):
        k0 = pl.multiple_of(j * tk, tk)
        kt = k_ref[pl.ds(k0, tk), t * LANES:(t + 1) * LANES]
        return lax.dot_general(kt, lhs_ref[t], _NT, preferred_element_type=F32)

    def pipelined(j, consume):
        raw = raw_scores(j, 0)
        for t in range(n_tiles):
            nxt = raw_scores(j, t + 1) if t + 1 < n_tiles else None
            consume(t, raw)
            raw = nxt

    def tile_step(j, bias_ref, first):
        gap = ((i - j) * tq).astype(F32)

        def consume(t, raw):
            for hh in range(2):
                head = 2 * t + hh
                off = -slopes[head] * gap
                p_parts = []
                alphas = []
                for c in range(2):
                    cols = slice((2 * hh + c) * tq, (2 * hh + c + 1) * tq)
                    mcols = slice(c * tq, (c + 1) * tq)
                    s = raw[:, cols] + bias_ref[head]
                    s_max = jnp.max(s, axis=0, keepdims=True)
                    if first:
                        m_new = s_max
                        shift = s_max
                    else:
                        m_old = m_ref[head, :, mcols]
                        m_new = jnp.maximum(m_old, s_max + off)
                        shift = m_new - off
                        alphas.append(jnp.exp2(m_old - m_new))
                    m_ref[head, :, mcols] = m_new
                    p_parts.append(jnp.exp2(s - shift).astype(BF16))
                pv = jnp.dot(vt_ref[j, head], jnp.concatenate(p_parts, axis=1),
                             preferred_element_type=F32)
                if first:
                    acc_ref[head] = pv
                else:
                    acc_ref[head] = jnp.concatenate(alphas, axis=1) * acc_ref[head] + pv

        pipelined(j, consume)

    def fast_step(j, carry):
        gap = ((i - j) * tq).astype(F32)

        def consume(t, raw):
            for hh in range(2):
                head = 2 * t + hh
                row = -slopes[head] * gap - m_ref[head]
                bias = boff_ref[head]
                p_parts = []
                for c in range(2):
                    cols = slice((2 * hh + c) * tq, (2 * hh + c + 1) * tq)
                    p_parts.append(
                        jnp.exp2((raw[:, cols] + bias) + row[:, c * tq:(c + 1) * tq]).astype(BF16))
                acc_ref[head] += jnp.dot(vt_ref[j, head], jnp.concatenate(p_parts, axis=1),
                                         preferred_element_type=F32)

        pipelined(j, consume)
        return carry

    def robust_step(j, carry):
        tile_step(j, boff_ref, False)
        return carry

    tile_step(i, bdiag_ref, True)
    lax.fori_loop(0, i, fast_step, 0)

    l_max = acc_ref[0, DIFF_VDIM:DIFF_VDIM + 1, :]
    for h in range(1, DIFF_HEADS):
        l_max = jnp.maximum(l_max, acc_ref[h, DIFF_VDIM:DIFF_VDIM + 1, :])
    bound = jnp.max(l_max, axis=1, keepdims=True) * vmax_ref[...]
    overflowed = jnp.logical_not(bound[0, 0] < ATTN_SAFE_SUM)

    @pl.when(overflowed)
    def _():
        tile_step(i, bdiag_ref, True)
        lax.fori_loop(0, i, robust_step, 0)

    lam = (jnp.exp(jnp.sum(lq1_ref[...] * lk1_ref[...], axis=-1, keepdims=True))
           - jnp.exp(jnp.sum(lq2_ref[...] * lk2_ref[...], axis=-1, keepdims=True))
           + lam_init)
    for t in range(n_tiles):
        halves = []
        for hh in range(2):
            head = 2 * t + hh
            inv = 1.0 / acc_ref[head, DIFF_VDIM:DIFF_VDIM + 1, :]
            o0 = acc_ref[head, 0:DIFF_VDIM, 0:tq] * inv[:, 0:tq]
            o1 = acc_ref[head, 0:DIFF_VDIM, tq:2 * tq] * (lam * inv[:, tq:2 * tq])
            o = o0 - o1
            ms = jnp.mean(o * o, axis=0, keepdims=True)
            halves.append(o * lax.rsqrt(ms + EPS))
        o_t = jnp.concatenate(halves, axis=0) * g_ref[t * LANES:(t + 1) * LANES, :]
        o_ref[:, t * LANES:(t + 1) * LANES] = (o_t.T * (1.0 - lam_init)).astype(BF16)


def _attn(qkv, lq1, lk1, lq2, lk2, subln_g, layer_idx, bsz, seq):
    tq = min(ATTN_TQ, seq)
    nq = seq // tq
    small = lambda: pl.BlockSpec((1, DIFF_QKDIM), lambda b, i: (0, 0))
    return pl.pallas_call(
        functools.partial(_attn_kernel, layer_idx=layer_idx, tq=tq, seq=seq),
        grid=(bsz, nq),
        in_specs=[
            pl.BlockSpec((tq, DIFF_WIDTH), lambda b, i: (b * nq + i, 0)),
            pl.BlockSpec((seq, DIFF_WIDTH), lambda b, i: (b, 1)),
            pl.BlockSpec((seq, DIFF_WIDTH), lambda b, i: (b, 2)),
            small(), small(), small(), small(),
            pl.BlockSpec((DIFF_WIDTH, 1), lambda b, i: (0, 0)),
        ],
        out_specs=pl.BlockSpec((tq, DIFF_WIDTH), lambda b, i: (b * nq + i, 0)),
        out_shape=jax.ShapeDtypeStruct((bsz * seq, DIFF_WIDTH), BF16),
        scratch_shapes=[
            pltpu.VMEM((seq // tq, DIFF_HEADS, ATTN_VT_ROWS, tq), BF16),
            pltpu.VMEM((1, 1), F32),
            pltpu.VMEM((DIFF_HEADS, tq, tq), F32),
            pltpu.VMEM((DIFF_HEADS, tq, tq), F32),
            pltpu.VMEM((DIFF_WIDTH // LANES, 4 * tq, LANES), BF16),
            pltpu.VMEM((DIFF_HEADS, 1, 2 * tq), F32),
            pltpu.VMEM((DIFF_HEADS, ATTN_VT_ROWS, 2 * tq), F32),
        ],
        compiler_params=pltpu.CompilerParams(
            dimension_semantics=("arbitrary", "arbitrary"),
            vmem_limit_bytes=VMEM_LIMIT_BYTES),
        name="attn",
    )(qkv, qkv, qkv, lq1.reshape(1, -1), lk1.reshape(1, -1), lq2.reshape(1, -1), lk2.reshape(1, -1),
      jnp.tile(subln_g, DIFF_HEADS).reshape(DIFF_WIDTH, 1))


def _hgrn_kernel(qq_ref, vv_ref, gs_ref, lf_ref, kk_ref, ng_ref, o_ref, st_ref, *, rows):
    w = HGRN_WIDTH
    c = CHUNK

    @pl.when(pl.program_id(1) == 0)
    def _():
        st_ref[...] = jnp.zeros_like(st_ref)

    row = lax.broadcasted_iota(jnp.int32, (c, w), 0)
    lane_w = lax.broadcasted_iota(jnp.int32, (HGRN_SUB, w), 1)
    bd_mask = (lax.broadcasted_iota(jnp.int32, (LANES, LANES), 0) // HGRN_DIM
               == lax.broadcasted_iota(jnp.int32, (LANES, LANES), 1) // HGRN_DIM)
    causal = (lax.broadcasted_iota(jnp.int32, (c, c), 1)
              <= lax.broadcasted_iota(jnp.int32, (c, c), 0))
    lane128 = lax.broadcasted_iota(jnp.int32, (c, LANES), 1)
    ones_bd = _head_ones(w, HGRN_DIM)
    n_sub = c // HGRN_SUB

    def prepare(ci):
        r0 = ci * c
        kk = kk_ref[r0:r0 + c, :]
        qq = qq_ref[r0:r0 + c, :].astype(F32)
        vv_b = vv_ref[r0:r0 + c, :]

        bc = lf_ref[r0:r0 + c, :]
        sh = 1
        while sh < c:
            bc = bc + jnp.where(row >= sh, pltpu.roll(bc, sh, axis=0), 0.0)
            sh *= 2

        blocks = []
        for si in range(n_sub):
            s0 = si * HGRN_SUB
            ref_row = bc[s0 + HGRN_SUB // 2:s0 + HGRN_SUB // 2 + 1, :]
            q_s = qq[s0:s0 + HGRN_SUB] * jnp.exp2(bc[s0:s0 + HGRN_SUB] - ref_row)
            n_keys = s0 + HGRN_SUB
            k_s = kk[0:n_keys] * jnp.exp2(ref_row - bc[0:n_keys])
            if n_keys < c:
                k_s = jnp.concatenate([k_s, jnp.zeros((c - n_keys, w), F32)], axis=0)
            k_s = k_s.astype(BF16)
            lhs = jnp.concatenate(
                [jnp.where(lane_w // HGRN_DIM == h, q_s, 0.0) for h in range(HGRN_HEADS)],
                axis=0).astype(BF16)
            blocks.append(lax.dot_general(lhs, k_s, _NT, preferred_element_type=F32))

        b_last = bc[c - 1:c, :]
        return dict(
            r0=r0, vv_b=vv_b, blocks=blocks,
            q_dec=(qq * jnp.exp2(bc)).astype(BF16),
            k2=(kk * jnp.exp2(b_last - bc)).astype(BF16),
            vv_t=vv_b.astype(F32).T.astype(BF16),
            carry_dec=jnp.exp2(b_last))

    def advance_state(ch):
        o_inter_tiles = []
        for t in range(w // LANES):
            tl = slice(t * LANES, (t + 1) * LANES)
            st = st_ref[t]
            o_inter_tiles.append(lax.dot_general(ch["q_dec"][:, tl], st.astype(BF16), _NT,
                                                 preferred_element_type=F32))
            upd = jnp.dot(ch["vv_t"][tl], ch["k2"][:, tl], preferred_element_type=F32)
            st_ref[t] = st * ch["carry_dec"][:, tl] + jnp.where(bd_mask, upd, 0.0)
        return jnp.concatenate(o_inter_tiles, axis=1)

    def finish(ch, o_inter):
        o_tiles = []
        for t in range(w // LANES):
            v_pair = ch["vv_b"][:, t * LANES:(t + 1) * LANES]
            res = []
            for hh in range(2):
                h = 2 * t + hh
                a_h = jnp.concatenate(
                    [ch["blocks"][si][h * HGRN_SUB:(h + 1) * HGRN_SUB] for si in range(n_sub)], axis=0)
                a_h = jnp.where(causal, a_h, 0.0).astype(BF16)
                res.append(jnp.dot(a_h, v_pair, preferred_element_type=F32))
            o_tiles.append(jnp.where(lane128 < HGRN_DIM, res[0], res[1]))
        o = jnp.concatenate(o_tiles, axis=1) + o_inter
        ms = _split_dot(o * o, ones_bd) * (1.0 / HGRN_DIM)
        o = o * lax.rsqrt(ms + EPS) * ng_ref[...]
        r0 = ch["r0"]
        o_ref[r0:r0 + c, :] = (o * gs_ref[r0:r0 + c, :].astype(F32)).astype(BF16)

    n_chunks = rows // c
    ch = prepare(0)
    for ci in range(n_chunks):
        nxt = prepare(ci + 1) if ci + 1 < n_chunks else None
        o_inter = advance_state(ch)
        finish(ch, o_inter)
        ch = nxt


def _hgrn(h_b, h_f, norm_g, bsz, seq):
    rows = min(HGRN_ROWS, seq)
    ns = seq // rows
    w = HGRN_WIDTH
    col = lambda j: pl.BlockSpec((rows, w), lambda b, s, j=j: (b * ns + s, j))
    return pl.pallas_call(
        functools.partial(_hgrn_kernel, rows=rows),
        grid=(bsz, ns),
        in_specs=[col(0), col(1), col(2), col(0), col(1),
                  pl.BlockSpec((1, w), lambda b, s: (0, 0))],
        out_specs=pl.BlockSpec((rows, w), lambda b, s: (b * ns + s, 0)),
        out_shape=jax.ShapeDtypeStruct((bsz * seq, w), BF16),
        scratch_shapes=[pltpu.VMEM((w // LANES, LANES, LANES), F32)],
        compiler_params=pltpu.CompilerParams(
            dimension_semantics=("arbitrary", "arbitrary"),
            vmem_limit_bytes=VMEM_LIMIT_BYTES),
        name="hgrn",
    )(h_b, h_b, h_b, h_f, h_f, jnp.tile(norm_g, HGRN_HEADS).reshape(1, w))


def _ffn_kernel(x_ref, ya_ref, yb_ref, yc_ref, wo_ref, g_ref, wup_ref, wgate_ref, cw_ref, cb_ref,
                wdown_ref, fg_ref, o_ref, up_ref, *, rows, final):
    s = pl.program_id(1)

    @pl.when(s == 0)
    def _():
        up_ref[0:SUBLANES, :] = jnp.zeros((SUBLANES, D_FF), F32)

    mix = jnp.concatenate([ya_ref[...], yb_ref[...], yc_ref[...]], axis=-1).astype(BF16)
    x1 = x_ref[...] + jnp.dot(mix, wo_ref[...], preferred_element_type=F32)
    h = _rms(x1, g_ref[...]).astype(BF16)
    up = jnp.dot(h, wup_ref[...], preferred_element_type=F32)
    gate = jnp.dot(h, wgate_ref[...], preferred_element_type=F32)

    up_ref[SUBLANES:SUBLANES + rows, :] = up
    conv = (cw_ref[2:3, :] * up
            + cw_ref[1:2, :] * up_ref[SUBLANES - 1:SUBLANES - 1 + rows, :]
            + cw_ref[0:1, :] * up_ref[SUBLANES - 2:SUBLANES - 2 + rows, :]
            + cb_ref[...])
    up_ref[0:SUBLANES, :] = up[rows - SUBLANES:rows, :]
    act = (_gelu_tanh(conv) * gate).astype(BF16)
    x2 = x1 + jnp.dot(act, wdown_ref[...], preferred_element_type=F32)
    if final:
        x2 = _rms(x2, fg_ref[...])
    o_ref[...] = x2


def _ffn(x2d, ya_tm, yb, yc, w_out, g, w_up, w_gate, conv_w, conv_b, w_down, final_g, final, bsz, seq):
    rows = min(FFN_ROWS, seq)
    ns = seq // rows
    const = lambda shape: pl.BlockSpec(shape, lambda b, s: (0, 0), pipeline_mode=pl.Buffered(1))
    tok = lambda width: pl.BlockSpec((rows, width), lambda b, s: (b * ns + s, 0))
    return pl.pallas_call(
        functools.partial(_ffn_kernel, rows=rows, final=final),
        grid=(bsz, ns),
        in_specs=[
            tok(D_MODEL),
            pl.BlockSpec((rows, S5_WIDTH), lambda b, s: (s, b)),
            tok(DIFF_WIDTH), tok(HGRN_WIDTH),
            const((D_MODEL, D_MODEL)), const((1, D_MODEL)),
            const((D_MODEL, D_FF)), const((D_MODEL, D_FF)),
            const((3, D_FF)), const((1, D_FF)),
            const((D_FF, D_MODEL)), const((1, D_MODEL)),
        ],
        out_specs=tok(D_MODEL),
        out_shape=jax.ShapeDtypeStruct((bsz * seq, D_MODEL), F32),
        scratch_shapes=[pltpu.VMEM((SUBLANES + rows, D_FF), F32)],
        compiler_params=pltpu.CompilerParams(
            dimension_semantics=("arbitrary", "arbitrary"),
            vmem_limit_bytes=VMEM_LIMIT_BYTES),
        name="ffn",
    )(x2d, ya_tm, yb, yc, w_out, g, w_up, w_gate, conv_w, conv_b, w_down, final_g)


def kernel(x, norm_mix_g, w_in, s5_lambda_re, s5_lambda_im, s5_log_step, s5_b_re, s5_b_im, s5_c_re,
           s5_c_im, s5_d, s5_w_glu, diff_lambda_q1, diff_lambda_k1, diff_lambda_q2, diff_lambda_k2,
           diff_subln_g, hgrn_lb_logits, hgrn_norm_g, w_out, norm_ffn_g, w_up, w_gate, conv_w, conv_b,
           w_down, final_norm_g):
    bsz, seq, d = x.shape
    assert d == D_MODEL and seq % CHUNK == 0 and bsz % SUBLANES == 0
    x2d = x.reshape(bsz * seq, d)
    for l in range(DEPTH):
        u_tm, qkv, h_b, h_f = _proj(x2d, norm_mix_g[l].reshape(1, d), w_in[l].astype(BF16),
                                    hgrn_lb_logits, l, bsz, seq)
        ya_tm = _s5(u_tm.reshape(seq * bsz, S5_WIDTH), s5_lambda_re[l], s5_lambda_im[l], s5_log_step[l],
                    s5_b_re[l], s5_b_im[l], s5_c_re[l], s5_c_im[l], s5_d[l], s5_w_glu[l].astype(BF16),
                    bsz, seq)
        yb = _attn(qkv, diff_lambda_q1[l], diff_lambda_k1[l], diff_lambda_q2[l], diff_lambda_k2[l],
                   diff_subln_g[l], l, bsz, seq)
        yc = _hgrn(h_b, h_f, hgrn_norm_g[l], bsz, seq)
        x2d = _ffn(x2d, ya_tm.reshape(seq, bsz * S5_WIDTH), yb, yc, w_out[l].astype(BF16),
                   norm_ffn_g[l].reshape(1, d), w_up[l].astype(BF16), w_gate[l].astype(BF16),
                   conv_w[l], conv_b[l].reshape(1, D_FF), w_down[l].astype(BF16),
                   final_norm_g.reshape(1, d), l == DEPTH - 1, bsz, seq)
    return x2d.reshape(bsz, seq, d)
```

```python
import functools
import math

import jax
import jax.numpy as jnp
import numpy as np
from jax import lax
from jax.experimental import pallas as pl
from jax.experimental.pallas import tpu as pltpu

F32 = jnp.float32
BF16 = jnp.bfloat16

D_MODEL = 1024
DEPTH = 2
CHUNK = 64
EPS = 1e-6
MASK_VALUE = -1e30
MIN_FORGET = 1e-6
S5_WIDTH = 256
DIFF_WIDTH = 384
HGRN_WIDTH = 384
S5_GROUP = 16
S5_GROUPS = 16
S5_STATE = 64
S5_LANES = S5_GROUPS * S5_STATE
DIFF_HEADS = 6
DIFF_VDIM = 64
DIFF_QKDIM = 32
HGRN_HEADS = 6
HGRN_DIM = 64
D_FF = 2816
QKV_WIDTH = 3 * DIFF_WIDTH
HG_WIDTH = 4 * HGRN_WIDTH
D_IN = S5_WIDTH + QKV_WIDTH + HG_WIDTH

LANES = 128
SUBLANES = 8
VMEM_LIMIT_BYTES = 56 * 1024 * 1024

PROJ_ROWS = 512
PROJ_PARTS = 2
FFN_ROWS = 512
FFN_PARTS = 2
S5_STEPS = 64
S5_LANE_CHUNK = 512
ATTN_TQ = 256
HGRN_ROWS = 512
HGRN_SUB = 8

_NT = (((1,), (1,)), ((), ()))
LOG2E = math.log2(math.e)
ATTN_Q_SCALE = DIFF_QKDIM ** -0.5 * LOG2E
ATTN_SAFE_SUM = 2.0 ** 120
ATTN_VT_ROWS = DIFF_VDIM + 16


def _rms(x, g):
    ms = jnp.mean(x * x, axis=-1, keepdims=True)
    return x * lax.rsqrt(ms + EPS) * g


def _gelu_tanh(x):
    c = math.sqrt(2.0 / math.pi)
    return 0.5 * x * (1.0 + jnp.tanh(c * (x + 0.044715 * (x * x * x))))


def _sigmoid(x):
    return 0.5 * jnp.tanh(0.5 * x) + 0.5


def _split_dot(x, ones_bd):
    hi = x.astype(BF16)
    lo = (x - hi.astype(F32)).astype(BF16)
    return (jnp.dot(hi, ones_bd, preferred_element_type=F32)
            + jnp.dot(lo, ones_bd, preferred_element_type=F32))


def _head_ones(width, head):
    r = lax.broadcasted_iota(jnp.int32, (width, width), 0) // head
    c = lax.broadcasted_iota(jnp.int32, (width, width), 1) // head
    return jnp.where(r == c, 1.0, 0.0).astype(BF16)


def _hgrn_lower_bound(logits, layer_idx):
    e = jnp.exp(logits - jnp.max(logits, axis=0, keepdims=True))
    p = e / jnp.sum(e, axis=0, keepdims=True)
    return jnp.sum(p[0:layer_idx + 1], axis=0, keepdims=True) - p[0:1]


def _proj_kernel(x_ref, g_ref, w_ref, lbl_ref, u_ref, qkv_ref, hb_ref, hf_ref, *, layer_idx, rows):
    w = HGRN_WIDTH
    q0 = S5_WIDTH
    k0 = S5_WIDTH + DIFF_WIDTH
    lb = _hgrn_lower_bound(lbl_ref[...], layer_idx)
    parts = [slice(p * (rows // PROJ_PARTS), (p + 1) * (rows // PROJ_PARTS)) for p in range(PROJ_PARTS)]
    hs = [_rms(x_ref[r, :], g_ref[...]).astype(BF16) for r in parts]
    for r, h in zip(parts, hs):
        u_ref[r, :] = jnp.dot(h, w_ref[:, 0:S5_WIDTH], preferred_element_type=F32).astype(BF16)
        qkv_ref[r, 0:DIFF_WIDTH] = (jnp.dot(h, w_ref[:, q0:k0], preferred_element_type=F32)
                                    * ATTN_Q_SCALE).astype(BF16)
        qkv_ref[r, DIFF_WIDTH:QKV_WIDTH] = jnp.dot(h, w_ref[:, k0:q0 + QKV_WIDTH],
                                                   preferred_element_type=F32).astype(BF16)
        hg = jnp.dot(h, w_ref[:, S5_WIDTH + QKV_WIDTH:D_IN], preferred_element_type=F32)
        q = hg[:, 0:w]
        f = lb + (1.0 - lb) * _sigmoid(hg[:, w:2 * w])
        g = hg[:, 3 * w:4 * w]
        hb_ref[r, 0:w] = (q * _sigmoid(q)).astype(BF16)
        hb_ref[r, w:2 * w] = hg[:, 2 * w:3 * w].astype(BF16)
        hb_ref[r, 2 * w:3 * w] = (g * _sigmoid(g)).astype(BF16)
        hf_ref[r, 0:w] = jnp.log(jnp.maximum(f, MIN_FORGET)) * LOG2E
        hf_ref[r, w:2 * w] = 1.0 - f


def _proj(x2d, g, w_bf16, lb_logits, layer_idx, bsz, seq):
    tt = min(PROJ_ROWS, seq)
    ns = seq // tt
    tok = lambda width: pl.BlockSpec((tt, width), lambda b, s: (b * ns + s, 0))
    return pl.pallas_call(
        functools.partial(_proj_kernel, layer_idx=layer_idx, rows=tt),
        grid=(bsz, ns),
        in_specs=[
            tok(D_MODEL),
            pl.BlockSpec((1, D_MODEL), lambda b, s: (0, 0)),
            pl.BlockSpec((D_MODEL, D_IN), lambda b, s: (0, 0)),
            pl.BlockSpec((DEPTH, HGRN_WIDTH), lambda b, s: (0, 0)),
        ],
        out_specs=[
            pl.BlockSpec((tt, S5_WIDTH), lambda b, s: (s, b)),
            tok(QKV_WIDTH), tok(3 * HGRN_WIDTH), tok(2 * HGRN_WIDTH),
        ],
        out_shape=[
            jax.ShapeDtypeStruct((seq, bsz * S5_WIDTH), BF16),
            jax.ShapeDtypeStruct((bsz * seq, QKV_WIDTH), BF16),
            jax.ShapeDtypeStruct((bsz * seq, 3 * HGRN_WIDTH), BF16),
            jax.ShapeDtypeStruct((bsz * seq, 2 * HGRN_WIDTH), F32),
        ],
        compiler_params=pltpu.CompilerParams(
            dimension_semantics=("arbitrary", "arbitrary"),
            vmem_limit_bytes=VMEM_LIMIT_BYTES),
        name="proj",
    )(x2d, g, w_bf16, lb_logits)


def _s5_kernel(u_ref, lre_ref, lim_ref, lstep_ref, bre_ref, bim_ref, cre_ref, cim_ref,
               d_ref, wglu_ref, o_ref, bb_ref, cc_ref, lam_ref, x_ref, bu_ref, *, bsz, steps):
    n = S5_LANES

    @pl.when(pl.program_id(0) == 0)
    def _():
        lre = lre_ref[...]
        lim = lim_ref[...]
        dt = jnp.exp(lstep_ref[...])
        mag = jnp.exp(lre * dt)
        ang = lim * dt
        lb_re = mag * jnp.cos(ang)
        lb_im = mag * jnp.sin(ang)
        den = lre * lre + lim * lim
        num_re = lb_re - 1.0
        num_im = lb_im
        coef_re = (num_re * lre + num_im * lim) / den
        coef_im = (num_im * lre - num_re * lim) / den
        b_re = bre_ref[...]
        b_im = bim_ref[...]
        bb_ref[:, 0:n] = (coef_re * b_re - coef_im * b_im).astype(BF16)
        bb_ref[:, n:2 * n] = (coef_re * b_im + coef_im * b_re).astype(BF16)
        cc_ref[0:n, :] = cre_ref[...].astype(BF16)
        cc_ref[n:2 * n, :] = (-cim_ref[...]).astype(BF16)
        lam_ref[0:1, :] = lb_re
        lam_ref[1:2, :] = lb_im
        x_ref[...] = jnp.zeros_like(x_ref)

    u_b = u_ref[...]
    u = u_b.astype(F32)
    half =(steps // 2) * bsz
    bu_ref[0:half, :] = jnp.dot(u_b[0:half], bb_ref[...], preferred_element_type=F32)
    bu_ref[half:, :] = jnp.dot(u_b[half:], bb_ref[...], preferred_element_type=F32)

    for c0 in range(0, n, S5_LANE_CHUNK):
        cw = S5_LANE_CHUNK
        a_re = jnp.broadcast_to(lam_ref[0:1, c0:c0 + cw], (bsz, cw))
        a_im = jnp.broadcast_to(lam_ref[1:2, c0:c0 + cw], (bsz, cw))

        def step(t, carry, c0=c0, cw=cw, a_re=a_re, a_im=a_im):
            x_re, x_im = carry
            r0 = pl.multiple_of(t * bsz, bsz)
            n_re = a_re * x_re - a_im * x_im + bu_ref[pl.ds(r0, bsz), c0:c0 + cw]
            n_im = a_re * x_im + a_im * x_re + bu_ref[pl.ds(r0, bsz), n + c0:n + c0 + cw]
            bu_ref[pl.ds(r0, bsz), c0:c0 + cw] = n_re
            bu_ref[pl.ds(r0, bsz), n + c0:n + c0 + cw] = n_im
            return n_re, n_im

        x_re, x_im = lax.fori_loop(
            0, steps, step, (x_ref[:, c0:c0 + cw], x_ref[:, n + c0:n + c0 + cw]))
        x_ref[:, c0:c0 + cw] = x_re
        x_ref[:, n + c0:n + c0 + cw] = x_im

    y = jnp.concatenate(
        [jnp.dot(bu_ref[0:half, :].astype(BF16), cc_ref[...], preferred_element_type=F32),
         jnp.dot(bu_ref[half:, :].astype(BF16), cc_ref[...], preferred_element_type=F32)],
        axis=0) + d_ref[...] * u
    z = _gelu_tanh(y)
    gate = jnp.dot(z.astype(BF16), wglu_ref[...], preferred_element_type=F32)
    o_ref[...] = (z * _sigmoid(gate)).astype(BF16)


def _block_diag(blocks):
    g, r, c = blocks.shape
    out = jnp.zeros((g, r, g, c), blocks.dtype)
    idx = jnp.arange(g)
    out = out.at[idx, :, idx, :].set(blocks)
    return out.reshape(g * r, g * c)


def _s5(u_tm, lam_re, lam_im, log_step, b_re, b_im, c_re, c_im, d, w_glu_bf16, bsz, seq):
    steps = min(S5_STEPS, seq)
    rows = steps * bsz
    n = S5_LANES
    lre = lam_re.reshape(1, n)
    lim = lam_im.reshape(1, n)
    lstep = jnp.repeat(log_step, S5_STATE).reshape(1, n)
    bre_big = _block_diag(jnp.transpose(b_re, (0, 2, 1)))
    bim_big = _block_diag(jnp.transpose(b_im, (0, 2, 1)))
    cre_big = _block_diag(jnp.transpose(c_re, (0, 2, 1)))
    cim_big = _block_diag(jnp.transpose(c_im, (0, 2, 1)))
    full = lambda shape: pl.BlockSpec(shape, lambda i: (0, 0))
    return pl.pallas_call(
        functools.partial(_s5_kernel, bsz=bsz, steps=steps),
        grid=(seq // steps,),
        in_specs=[
            pl.BlockSpec((rows, S5_WIDTH), lambda i: (i, 0)),
            full((1, n)), full((1, n)), full((1, n)),
            full((S5_WIDTH, n)), full((S5_WIDTH, n)),
            full((n, S5_WIDTH)), full((n, S5_WIDTH)),
            full((1, S5_WIDTH)), full((S5_WIDTH, S5_WIDTH)),
        ],
        out_specs=pl.BlockSpec((rows, S5_WIDTH), lambda i: (i, 0)),
        out_shape=jax.ShapeDtypeStruct((seq * bsz, S5_WIDTH), BF16),
        scratch_shapes=[
            pltpu.VMEM((S5_WIDTH, 2 * n), BF16),
            pltpu.VMEM((2 * n, S5_WIDTH), BF16),
            pltpu.VMEM((2, n), F32),
            pltpu.VMEM((bsz, 2 * n), F32),
            pltpu.VMEM((rows, 2 * n), F32),
        ],
        compiler_params=pltpu.CompilerParams(
            dimension_semantics=("arbitrary",),
            vmem_limit_bytes=VMEM_LIMIT_BYTES),
        name="s5",
    )(u_tm, lre, lim, lstep, bre_big, bim_big, cre_big, cim_big, d.reshape(1, S5_WIDTH), w_glu_bf16)


def _attn_kernel(q_ref, k_ref, v_ref, lq1_ref, lk1_ref, lq2_ref, lk2_ref, g_ref, o_ref,
                 vt_ref, vmax_ref, boff_ref, bdiag_ref, lhs_ref, m_ref, acc_ref, *, layer_idx, tq, seq):
    i = pl.program_id(1)
    tk = tq
    lam_init = 0.8 - 0.6 * math.exp(-0.3 * layer_idx)
    n_tiles = DIFF_WIDTH // LANES
    groups = LANES // DIFF_QKDIM
    slopes = [LOG2E * 2.0 ** (-8.0 * (h + 1) / DIFF_HEADS) for h in range(DIFF_HEADS)]

    @pl.when((pl.program_id(0) == 0) & (i == 0))
    def _():
        r = lax.broadcasted_iota(jnp.int32, (tk, tq), 0)
        c = lax.broadcasted_iota(jnp.int32, (tk, tq), 1)
        rel = (c - r).astype(F32)
        visible = (r // CHUNK) <= (c // CHUNK)
        for h in range(DIFF_HEADS):
            boff_ref[h] = -slopes[h] * rel
            bdiag_ref[h] = jnp.where(visible, -slopes[h] * jnp.abs(rel), MASK_VALUE)

    @pl.when(i == 0)
    def _():
        extra = lax.broadcasted_iota(jnp.int32, (ATTN_VT_ROWS - DIFF_VDIM, tk), 0)
        ones_row = jnp.where(extra == 0, 1.0, 0.0).astype(BF16)
        vmax_ref[...] = jnp.max(jnp.abs(v_ref[...].astype(F32)), axis=(0, 1), keepdims=True)
        for j in range(seq // tk):
            v_t = v_ref[j * tk:(j + 1) * tk, :].astype(F32).T.astype(BF16)
            for h in range(DIFF_HEADS):
                vt_ref[j, h, 0:DIFF_VDIM, :] = v_t[h * DIFF_VDIM:(h + 1) * DIFF_VDIM]
                vt_ref[j, h, DIFF_VDIM:ATTN_VT_ROWS, :] = ones_row

    lane = lax.broadcasted_iota(jnp.int32, (tq, LANES), 1)
    for t in range(n_tiles):
        qt = q_ref[:, t * LANES:(t + 1) * LANES]
        for g in range(groups):
            lhs_ref[t, g * tq:(g + 1) * tq, :] = jnp.where(lane // DIFF_QKDIM == g, qt, jnp.zeros_like(qt))

    def raw_scores(j, t):
        k0 = pl.multiple_of(j * tk, tk)
        kt = k_ref[pl.ds(k0, tk), t * LANES:(t + 1) * LANES]
        return lax.dot_general(kt, lhs_ref[t], _NT, preferred_element_type=F32)

    def pipelined(j, consume):
        raw = raw_scores(j, 0)
        for t in range(n_tiles):
            nxt = raw_scores(j, t + 1) if t + 1 < n_tiles else None
            consume(t, raw)
            raw = nxt

    def tile_step(j, bias_ref, first):
        gap = ((i - j) * tq).astype(F32)

        def consume(t, raw):
            for hh in range(2):
                head = 2 * t + hh
                off = -slopes[head] * gap
                p_parts = []
                alphas = []
                for c in range(2):
                    cols = slice((2 * hh + c) * tq, (2 * hh + c + 1) * tq)
                    mcols = slice(c * tq, (c + 1) * tq)
                    s = raw[:, cols] + bias_ref[head]
                    s_max = jnp.max(s, axis=0, keepdims=True)
                    if first:
                        m_new = s_max
                        shift = s_max
                    else:
                        m_old = m_ref[head, :, mcols]
                        m_new = jnp.maximum(m_old, s_max + off)
                        shift = m_new - off
                        alphas.append(jnp.exp2(m_old - m_new))
                    m_ref[head, :, mcols] = m_new
                    p_parts.append(jnp.exp2(s - shift).astype(BF16))
                pv = jnp.dot(vt_ref[j, head], jnp.concatenate(p_parts, axis=1),
                             preferred_element_type=F32)
                if first:
                    acc_ref[head] = pv
                else:
                    acc_ref[head] = jnp.concatenate(alphas, axis=1) * acc_ref[head] + pv

        pipelined(j, consume)

    def fast_step(j, carry):
        gap = ((i - j) * tq).astype(F32)

        def consume(t, raw):
            for hh in range(2):
                head = 2 * t + hh
                row = -slopes[head] * gap - m_ref[head]
                bias = boff_ref[head]
                p_parts = []
                for c in range(2):
                    cols = slice((2 * hh + c) * tq, (2 * hh + c + 1) * tq)
                    p_parts.append(
                        jnp.exp2((raw[:, cols] + bias) + row[:, c * tq:(c + 1) * tq]).astype(BF16))
                acc_ref[head] += jnp.dot(vt_ref[j, head], jnp.concatenate(p_parts, axis=1),
                                         preferred_element_type=F32)

        pipelined(j, consume)
        return carry

    def robust_step(j, carry):
        tile_step(j, boff_ref, False)
        return carry

    tile_step(i, bdiag_ref, True)
    lax.fori_loop(0, i, fast_step, 0)

    l_max = acc_ref[0, DIFF_VDIM:DIFF_VDIM + 1, :]
    for h in range(1, DIFF_HEADS):
        l_max = jnp.maximum(l_max, acc_ref[h, DIFF_VDIM:DIFF_VDIM + 1, :])
    bound = jnp.max(l_max, axis=1, keepdims=True) * vmax_ref[...]
    overflowed = jnp.logical_not(bound[0, 0] < ATTN_SAFE_SUM)

    @pl.when(overflowed)
    def _():
        tile_step(i, bdiag_ref, True)
        lax.fori_loop(0, i, robust_step, 0)

    lam = (jnp.exp(jnp.sum(lq1_ref[...] * lk1_ref[...], axis=-1, keepdims=True))
           - jnp.exp(jnp.sum(lq2_ref[...] * lk2_ref[...], axis=-1, keepdims=True))
           + lam_init)
    for t in range(n_tiles):
        halves = []
        for hh in range(2):
            head = 2 * t + hh
            inv = 1.0 / acc_ref[head, DIFF_VDIM:DIFF_VDIM + 1, :]
            o0 = acc_ref[head, 0:DIFF_VDIM, 0:tq] * inv[:, 0:tq]
            o1 = acc_ref[head, 0:DIFF_VDIM, tq:2 * tq] * (lam * inv[:, tq:2 * tq])
            o = o0 - o1
            ms = jnp.mean(o * o, axis=0, keepdims=True)
            halves.append(o * lax.rsqrt(ms + EPS))
        o_t = jnp.concatenate(halves, axis=0) * g_ref[t * LANES:(t + 1) * LANES, :]
        o_ref[:, t * LANES:(t + 1) * LANES] = (o_t.T * (1.0 - lam_init)).astype(BF16)


def _attn(qkv, lq1, lk1, lq2, lk2, subln_g, layer_idx, bsz, seq):
    tq = min(ATTN_TQ, seq)
    nq = seq // tq
    small = lambda: pl.BlockSpec((1, DIFF_QKDIM), lambda b, i: (0, 0))
    return pl.pallas_call(
        functools.partial(_attn_kernel, layer_idx=layer_idx, tq=tq, seq=seq),
        grid=(bsz, nq),
        in_specs=[
            pl.BlockSpec((tq, DIFF_WIDTH), lambda b, i: (b * nq + i, 0)),
            pl.BlockSpec((seq, DIFF_WIDTH), lambda b, i: (b, 1)),
            pl.BlockSpec((seq, DIFF_WIDTH), lambda b, i: (b, 2)),
            small(), small(), small(), small(),
            pl.BlockSpec((DIFF_WIDTH, 1), lambda b, i: (0, 0)),
        ],
        out_specs=pl.BlockSpec((tq, DIFF_WIDTH), lambda b, i: (b * nq + i, 0)),
        out_shape=jax.ShapeDtypeStruct((bsz * seq, DIFF_WIDTH), BF16),
        scratch_shapes=[
            pltpu.VMEM((seq // tq, DIFF_HEADS, ATTN_VT_ROWS, tq), BF16),
            pltpu.VMEM((1, 1), F32),
            pltpu.VMEM((DIFF_HEADS, tq, tq), F32),
            pltpu.VMEM((DIFF_HEADS, tq, tq), F32),
            pltpu.VMEM((DIFF_WIDTH // LANES, 4 * tq, LANES), BF16),
            pltpu.VMEM((DIFF_HEADS, 1, 2 * tq), F32),
            pltpu.VMEM((DIFF_HEADS, ATTN_VT_ROWS, 2 * tq), F32),
        ],
        compiler_params=pltpu.CompilerParams(
            dimension_semantics=("arbitrary", "arbitrary"),
            vmem_limit_bytes=VMEM_LIMIT_BYTES),
        name="attn",
    )(qkv, qkv, qkv, lq1.reshape(1, -1), lk1.reshape(1, -1), lq2.reshape(1, -1), lk2.reshape(1, -1),
      jnp.tile(subln_g, DIFF_HEADS).reshape(DIFF_WIDTH, 1))


def _hgrn_kernel(qq_ref, vv_ref, gs_ref, lf_ref, kk_ref, ng_ref, o_ref, st_ref, *, rows):
    w = HGRN_WIDTH
    c = CHUNK

    @pl.when(pl.program_id(1) == 0)
    def _():
        st_ref[...] = jnp.zeros_like(st_ref)

    row = lax.broadcasted_iota(jnp.int32, (c, w), 0)
    lane_w = lax.broadcasted_iota(jnp.int32, (HGRN_SUB, w), 1)
    bd_mask = (lax.broadcasted_iota(jnp.int32, (LANES, LANES), 0) // HGRN_DIM
               == lax.broadcasted_iota(jnp.int32, (LANES, LANES), 1) // HGRN_DIM)
    causal = (lax.broadcasted_iota(jnp.int32, (c, c), 1)
              <= lax.broadcasted_iota(jnp.int32, (c, c), 0))
    lane128 = lax.broadcasted_iota(jnp.int32, (c, LANES), 1)
    ones_bd = _head_ones(w, HGRN_DIM)
    n_sub = c // HGRN_SUB

    def prepare(ci):
        r0 = ci * c
        kk = kk_ref[r0:r0 + c, :]
        qq = qq_ref[r0:r0 + c, :].astype(F32)
        vv_b = vv_ref[r0:r0 + c, :]

        bc = lf_ref[r0:r0 + c, :]
        sh = 1
        while sh < c:
            bc = bc + jnp.where(row >= sh, pltpu.roll(bc, sh, axis=0), 0.0)
            sh *= 2

        blocks = []
        for si in range(n_sub):
            s0 = si * HGRN_SUB
            ref_row = bc[s0 + HGRN_SUB // 2:s0 + HGRN_SUB // 2 + 1, :]
            q_s = qq[s0:s0 + HGRN_SUB] * jnp.exp2(bc[s0:s0 + HGRN_SUB] - ref_row)
            n_keys = s0 + HGRN_SUB
            k_s = kk[0:n_keys] * jnp.exp2(ref_row - bc[0:n_keys])
            if n_keys < c:
                k_s = jnp.concatenate([k_s, jnp.zeros((c - n_keys, w), F32)], axis=0)
            k_s = k_s.astype(BF16)
            lhs = jnp.concatenate(
                [jnp.where(lane_w // HGRN_DIM == h, q_s, 0.0) for h in range(HGRN_HEADS)],
                axis=0).astype(BF16)
            blocks.append(lax.dot_general(lhs, k_s, _NT, preferred_element_type=F32))

        b_last = bc[c - 1:c, :]
        return dict(
            r0=r0, vv_b=vv_b, blocks=blocks,
            q_dec=(qq * jnp.exp2(bc)).astype(BF16),
            k2=(kk * jnp.exp2(b_last - bc)).astype(BF16),
            vv_t=vv_b.astype(F32).T.astype(BF16),
            carry_dec=jnp.exp2(b_last))

    def advance_state(ch):
        o_inter_tiles = []
        for t in range(w // LANES):
            tl = slice(t * LANES, (t + 1) * LANES)
            st = st_ref[t]
            o_inter_tiles.append(lax.dot_general(ch["q_dec"][:, tl], st.astype(BF16), _NT,
                                                 preferred_element_type=F32))
            upd = jnp.dot(ch["vv_t"][tl], ch["k2"][:, tl], preferred_element_type=F32)
            st_ref[t] = st * ch["carry_dec"][:, tl] + jnp.where(bd_mask, upd, 0.0)
        return jnp.concatenate(o_inter_tiles, axis=1)

    def finish(ch, o_inter):
        o_tiles = []
        for t in range(w // LANES):
            v_pair = ch["vv_b"][:, t * LANES:(t + 1) * LANES]
            res = []
            for hh in range(2):
                h = 2 * t + hh
                a_h = jnp.concatenate(
                    [ch["blocks"][si][h * HGRN_SUB:(h + 1) * HGRN_SUB] for si in range(n_sub)], axis=0)
                a_h = jnp.where(causal, a_h, 0.0).astype(BF16)
                res.append(jnp.dot(a_h, v_pair, preferred_element_type=F32))
            o_tiles.append(jnp.where(lane128 < HGRN_DIM, res[0], res[1]))
        o = jnp.concatenate(o_tiles, axis=1) + o_inter
        ms = _split_dot(o * o, ones_bd) * (1.0 / HGRN_DIM)
        o = o * lax.rsqrt(ms + EPS) * ng_ref[...]
        r0 = ch["r0"]
        o_ref[r0:r0 + c, :] = (o * gs_ref[r0:r0 + c, :].astype(F32)).astype(BF16)

    n_chunks = rows // c
    ch = prepare(0)
    for ci in range(n_chunks):
        nxt = prepare(ci + 1) if ci + 1 < n_chunks else None
        o_inter = advance_state(ch)
        finish(ch, o_inter)
        ch = nxt


def _hgrn(h_b, h_f, norm_g, bsz, seq):
    rows = min(HGRN_ROWS, seq)
    ns = seq // rows
    w = HGRN_WIDTH
    col = lambda j: pl.BlockSpec((rows, w), lambda b, s, j=j: (b * ns + s, j))
    return pl.pallas_call(
        functools.partial(_hgrn_kernel, rows=rows),
        grid=(bsz, ns),
        in_specs=[col(0), col(1), col(2), col(0), col(1),
                  pl.BlockSpec((1, w), lambda b, s: (0, 0))],
        out_specs=pl.BlockSpec((rows, w), lambda b, s: (b * ns + s, 0)),
        out_shape=jax.ShapeDtypeStruct((bsz * seq, w), BF16),
        scratch_shapes=[pltpu.VMEM((w // LANES, LANES, LANES), F32)],
        compiler_params=pltpu.CompilerParams(
            dimension_semantics=("arbitrary", "arbitrary"),
            vmem_limit_bytes=VMEM_LIMIT_BYTES),
        name="hgrn",
    )(h_b, h_b, h_b, h_f, h_f, jnp.tile(norm_g, HGRN_HEADS).reshape(1, w))


def _ffn_kernel(x_ref, ya_ref, yb_ref, yc_ref, wo_ref, g_ref, wup_ref, wgate_ref, cw_ref, cb_ref,
                wdown_ref, fg_ref, o_ref, up_ref, *, rows, final):
    s = pl.program_id(1)

    @pl.when(s == 0)
    def _():
        up_ref[0:SUBLANES, :] = jnp.zeros((SUBLANES, D_FF), F32)

    parts = [slice(p * (rows // FFN_PARTS), (p + 1) * (rows // FFN_PARTS)) for p in range(FFN_PARTS)]
    x1 = []
    for r in parts:
        mix = jnp.concatenate([ya_ref[r, :], yb_ref[r, :], yc_ref[r, :]], axis=-1).astype(BF16)
        x1.append(x_ref[r, :] + jnp.dot(mix, wo_ref[...], preferred_element_type=F32))
    gates = []
    for p, r in enumerate(parts):
        h = _rms(x1[p], g_ref[...]).astype(BF16)
        up_ref[SUBLANES + r.start:SUBLANES + r.stop, :] = jnp.dot(h, wup_ref[...],
                                                                  preferred_element_type=F32)
        gates.append(jnp.dot(h, wgate_ref[...], preferred_element_type=F32))
    for p, r in enumerate(parts):
        conv = (cw_ref[2:3, :] * up_ref[SUBLANES + r.start:SUBLANES + r.stop, :]
                + cw_ref[1:2, :] * up_ref[SUBLANES - 1 + r.start:SUBLANES - 1 + r.stop, :]
                + cw_ref[0:1, :] * up_ref[SUBLANES - 2 + r.start:SUBLANES - 2 + r.stop, :]
                + cb_ref[...])
        act = (_gelu_tanh(conv) * gates[p]).astype(BF16)
        x2 = x1[p] + jnp.dot(act, wdown_ref[...], preferred_element_type=F32)
        if final:
            x2 = _rms(x2, fg_ref[...])
        o_ref[r, :] = x2
    up_ref[0:SUBLANES, :] = up_ref[rows:rows + SUBLANES, :]


def _ffn(x2d, ya_tm, yb, yc, w_out, g, w_up, w_gate, conv_w, conv_b, w_down, final_g, final, bsz, seq):
    rows = min(FFN_ROWS, seq)
    ns = seq // rows
    const = lambda shape: pl.BlockSpec(shape, lambda b, s: (0, 0), pipeline_mode=pl.Buffered(1))
    tok = lambda width: pl.BlockSpec((rows, width), lambda b, s: (b * ns + s, 0))
    return pl.pallas_call(
        functools.partial(_ffn_kernel, rows=rows, final=final),
        grid=(bsz, ns),
        in_specs=[
            tok(D_MODEL),
            pl.BlockSpec((rows, S5_WIDTH), lambda b, s: (s, b)),
            tok(DIFF_WIDTH), tok(HGRN_WIDTH),
            const((D_MODEL, D_MODEL)), const((1, D_MODEL)),
            const((D_MODEL, D_FF)), const((D_MODEL, D_FF)),
            const((3, D_FF)), const((1, D_FF)),
            const((D_FF, D_MODEL)), const((1, D_MODEL)),
        ],
        out_specs=tok(D_MODEL),
        out_shape=jax.ShapeDtypeStruct((bsz * seq, D_MODEL), F32),
        scratch_shapes=[pltpu.VMEM((SUBLANES + rows, D_FF), F32)],
        compiler_params=pltpu.CompilerParams(
            dimension_semantics=("arbitrary", "arbitrary"),
            vmem_limit_bytes=VMEM_LIMIT_BYTES),
        name="ffn",
    )(x2d, ya_tm, yb, yc, w_out, g, w_up, w_gate, conv_w, conv_b, w_down, final_g)


def kernel(x, norm_mix_g, w_in, s5_lambda_re, s5_lambda_im, s5_log_step, s5_b_re, s5_b_im, s5_c_re,
           s5_c_im, s5_d, s5_w_glu, diff_lambda_q1, diff_lambda_k1, diff_lambda_q2, diff_lambda_k2,
           diff_subln_g, hgrn_lb_logits, hgrn_norm_g, w_out, norm_ffn_g, w_up, w_gate, conv_w, conv_b,
           w_down, final_norm_g):
    bsz, seq, d = x.shape
    assert d == D_MODEL and seq % CHUNK == 0 and bsz % SUBLANES == 0
    x2d = x.reshape(bsz * seq, d)
    for l in range(DEPTH):
        u_tm, qkv, h_b, h_f = _proj(x2d, norm_mix_g[l].reshape(1, d), w_in[l].astype(BF16),
                                    hgrn_lb_logits, l, bsz, seq)
        ya_tm = _s5(u_tm.reshape(seq * bsz, S5_WIDTH), s5_lambda_re[l], s5_lambda_im[l], s5_log_step[l],
                    s5_b_re[l], s5_b_im[l], s5_c_re[l], s5_c_im[l], s5_d[l], s5_w_glu[l].astype(BF16),
                    bsz, seq)
        yb = _attn(qkv, diff_lambda_q1[l], diff_lambda_k1[l], diff_lambda_q2[l], diff_lambda_k2[l],
                   diff_subln_g[l], l, bsz, seq)
        yc = _hgrn(h_b, h_f, hgrn_norm_g[l], bsz, seq)
        x2d = _ffn(x2d, ya_tm.reshape(seq, bsz * S5_WIDTH), yb, yc, w_out[l].astype(BF16),
                   norm_ffn_g[l].reshape(1, d), w_up[l].astype(BF16), w_gate[l].astype(BF16),
                   conv_w[l], conv_b[l].reshape(1, D_FF), w_down[l].astype(BF16),
                   final_norm_g.reshape(1, d), l == DEPTH - 1, bsz, seq)
    return x2d.reshape(bsz, seq, d)
```

```python
import functools
import math

import jax
import jax.numpy as jnp
import numpy as np
from jax import lax
from jax.experimental import pallas as pl
from jax.experimental.pallas import tpu as pltpu

F32 = jnp.float32
BF16 = jnp.bfloat16

D_MODEL = 1024
DEPTH = 2
CHUNK = 64
EPS = 1e-6
MASK_VALUE = -1e30
MIN_FORGET = 1e-6
S5_WIDTH = 256
DIFF_WIDTH = 384
HGRN_WIDTH = 384
S5_GROUP = 16
S5_GROUPS = 16
S5_STATE = 64
S5_LANES = S5_GROUPS * S5_STATE
DIFF_HEADS = 6
DIFF_VDIM = 64
DIFF_QKDIM = 32
HGRN_HEADS = 6
HGRN_DIM = 64
D_FF = 2816
QKV_WIDTH = 3 * DIFF_WIDTH
HG_WIDTH = 4 * HGRN_WIDTH
D_IN = S5_WIDTH + QKV_WIDTH + HG_WIDTH

LANES = 128
SUBLANES = 8
VMEM_LIMIT_BYTES = 56 * 1024 * 1024

PROJ_ROWS = 512
PROJ_PARTS = 2
FFN_ROWS = 512
FFN_PARTS = 2
S5_STEPS = 128
S5_LANE_CHUNK = 512
ATTN_TQ = 256
HGRN_ROWS = 512
HGRN_SUB = 8

_NT = (((1,), (1,)), ((), ()))
LOG2E = math.log2(math.e)
ATTN_Q_SCALE = DIFF_QKDIM ** -0.5 * LOG2E
ATTN_SAFE_SUM = 2.0 ** 120
ATTN_VT_ROWS = DIFF_VDIM + 16


def _rms(x, g):
    ms = jnp.mean(x * x, axis=-1, keepdims=True)
    return x * lax.rsqrt(ms + EPS) * g


def _gelu_tanh(x):
    c = math.sqrt(2.0 / math.pi)
    return 0.5 * x * (1.0 + jnp.tanh(c * (x + 0.044715 * (x * x * x))))


def _sigmoid(x):
    return 0.5 * jnp.tanh(0.5 * x) + 0.5


def _split_dot(x, ones_bd):
    hi = x.astype(BF16)
    lo = (x - hi.astype(F32)).astype(BF16)
    return (jnp.dot(hi, ones_bd, preferred_element_type=F32)
            + jnp.dot(lo, ones_bd, preferred_element_type=F32))


def _head_ones(width, head):
    r = lax.broadcasted_iota(jnp.int32, (width, width), 0) // head
    c = lax.broadcasted_iota(jnp.int32, (width, width), 1) // head
    return jnp.where(r == c, 1.0, 0.0).astype(BF16)


def _hgrn_lower_bound(logits, layer_idx):
    e = jnp.exp(logits - jnp.max(logits, axis=0, keepdims=True))
    p = e / jnp.sum(e, axis=0, keepdims=True)
    return jnp.sum(p[0:layer_idx + 1], axis=0, keepdims=True) - p[0:1]


def _proj_kernel(x_ref, g_ref, w_ref, lbl_ref, u_ref, qkv_ref, hb_ref, hf_ref, *, layer_idx, rows):
    w = HGRN_WIDTH
    q0 = S5_WIDTH
    k0 = S5_WIDTH + DIFF_WIDTH
    lb = _hgrn_lower_bound(lbl_ref[...], layer_idx)
    parts = [slice(p * (rows // PROJ_PARTS), (p + 1) * (rows // PROJ_PARTS)) for p in range(PROJ_PARTS)]
    hs = [_rms(x_ref[r, :], g_ref[...]).astype(BF16) for r in parts]
    for r, h in zip(parts, hs):
        u_ref[r, :] = jnp.dot(h, w_ref[:, 0:S5_WIDTH], preferred_element_type=F32).astype(BF16)
        qkv_ref[r, 0:DIFF_WIDTH] = (jnp.dot(h, w_ref[:, q0:k0], preferred_element_type=F32)
                                    * ATTN_Q_SCALE).astype(BF16)
        qkv_ref[r, DIFF_WIDTH:QKV_WIDTH] = jnp.dot(h, w_ref[:, k0:q0 + QKV_WIDTH],
                                                   preferred_element_type=F32).astype(BF16)
        hg = jnp.dot(h, w_ref[:, S5_WIDTH + QKV_WIDTH:D_IN], preferred_element_type=F32)
        q = hg[:, 0:w]
        f = lb + (1.0 - lb) * _sigmoid(hg[:, w:2 * w])
        g = hg[:, 3 * w:4 * w]
        hb_ref[r, 0:w] = (q * _sigmoid(q)).astype(BF16)
        hb_ref[r, w:2 * w] = hg[:, 2 * w:3 * w].astype(BF16)
        hb_ref[r, 2 * w:3 * w] = (g * _sigmoid(g)).astype(BF16)
        hf_ref[r, 0:w] = jnp.log(jnp.maximum(f, MIN_FORGET)) * LOG2E
        hf_ref[r, w:2 * w] = 1.0 - f


def _proj(x2d, g, w_bf16, lb_logits, layer_idx, bsz, seq):
    tt = min(PROJ_ROWS, seq)
    ns = seq // tt
    tok = lambda width: pl.BlockSpec((tt, width), lambda b, s: (b * ns + s, 0))
    return pl.pallas_call(
        functools.partial(_proj_kernel, layer_idx=layer_idx, rows=tt),
        grid=(bsz, ns),
        in_specs=[
            tok(D_MODEL),
            pl.BlockSpec((1, D_MODEL), lambda b, s: (0, 0)),
            pl.BlockSpec((D_MODEL, D_IN), lambda b, s: (0, 0)),
            pl.BlockSpec((DEPTH, HGRN_WIDTH), lambda b, s: (0, 0)),
        ],
        out_specs=[
            pl.BlockSpec((tt, S5_WIDTH), lambda b, s: (s, b)),
            tok(QKV_WIDTH), tok(3 * HGRN_WIDTH), tok(2 * HGRN_WIDTH),
        ],
        out_shape=[
            jax.ShapeDtypeStruct((seq, bsz * S5_WIDTH), BF16),
            jax.ShapeDtypeStruct((bsz * seq, QKV_WIDTH), BF16),
            jax.ShapeDtypeStruct((bsz * seq, 3 * HGRN_WIDTH), BF16),
            jax.ShapeDtypeStruct((bsz * seq, 2 * HGRN_WIDTH), F32),
        ],
        compiler_params=pltpu.CompilerParams(
            dimension_semantics=("arbitrary", "arbitrary"),
            vmem_limit_bytes=VMEM_LIMIT_BYTES),
        name="proj",
    )(x2d, g, w_bf16, lb_logits)


def _s5_kernel(u_ref, lre_ref, lim_ref, lstep_ref, bre_ref, bim_ref, cre_ref, cim_ref,
               d_ref, wglu_ref, o_ref, bb_ref, cc_ref, lam_ref, x_ref, bu_ref, *, bsz, steps):
    n = S5_LANES

    @pl.when(pl.program_id(0) == 0)
    def _():
        lre = lre_ref[...]
        lim = lim_ref[...]
        dt = jnp.exp(lstep_ref[...])
        mag = jnp.exp(lre * dt)
        ang = lim * dt
        lb_re = mag * jnp.cos(ang)
        lb_im = mag * jnp.sin(ang)
        den = lre * lre + lim * lim
        num_re = lb_re - 1.0
        num_im = lb_im
        coef_re = (num_re * lre + num_im * lim) / den
        coef_im = (num_im * lre - num_re * lim) / den
        b_re = bre_ref[...]
        b_im = bim_ref[...]
        bb_ref[:, 0:n] = (coef_re * b_re - coef_im * b_im).astype(BF16)
        bb_ref[:, n:2 * n] = (coef_re * b_im + coef_im * b_re).astype(BF16)
        cc_ref[0:n, :] = cre_ref[...].astype(BF16)
        cc_ref[n:2 * n, :] = (-cim_ref[...]).astype(BF16)
        lam_ref[0:1, :] = lb_re
        lam_ref[1:2, :] = lb_im
        x_ref[...] = jnp.zeros_like(x_ref)

    u_b = u_ref[...]
    u = u_b.astype(F32)
    half =(steps // 2) * bsz
    bu_ref[0:half, :] = jnp.dot(u_b[0:half], bb_ref[...], preferred_element_type=F32)
    bu_ref[half:, :] = jnp.dot(u_b[half:], bb_ref[...], preferred_element_type=F32)

    for c0 in range(0, n, S5_LANE_CHUNK):
        cw = S5_LANE_CHUNK
        a_re = jnp.broadcast_to(lam_ref[0:1, c0:c0 + cw], (bsz, cw))
        a_im = jnp.broadcast_to(lam_ref[1:2, c0:c0 + cw], (bsz, cw))

        def step(t, carry, c0=c0, cw=cw, a_re=a_re, a_im=a_im):
            x_re, x_im = carry
            r0 = pl.multiple_of(t * bsz, bsz)
            n_re = a_re * x_re - a_im * x_im + bu_ref[pl.ds(r0, bsz), c0:c0 + cw]
            n_im = a_re * x_im + a_im * x_re + bu_ref[pl.ds(r0, bsz), n + c0:n + c0 + cw]
            bu_ref[pl.ds(r0, bsz), c0:c0 + cw] = n_re
            bu_ref[pl.ds(r0, bsz), n + c0:n + c0 + cw] = n_im
            return n_re, n_im

        x_re, x_im = lax.fori_loop(
            0, steps, step, (x_ref[:, c0:c0 + cw], x_ref[:, n + c0:n + c0 + cw]))
        x_ref[:, c0:c0 + cw] = x_re
        x_ref[:, n + c0:n + c0 + cw] = x_im

    y = jnp.concatenate(
        [jnp.dot(bu_ref[0:half, :].astype(BF16), cc_ref[...], preferred_element_type=F32),
         jnp.dot(bu_ref[half:, :].astype(BF16), cc_ref[...], preferred_element_type=F32)],
        axis=0) + d_ref[...] * u
    z = _gelu_tanh(y)
    gate = jnp.dot(z.astype(BF16), wglu_ref[...], preferred_element_type=F32)
    o_ref[...] = (z * _sigmoid(gate)).astype(BF16)


def _block_diag(blocks):
    g, r, c = blocks.shape
    out = jnp.zeros((g, r, g, c), blocks.dtype)
    idx = jnp.arange(g)
    out = out.at[idx, :, idx, :].set(blocks)
    return out.reshape(g * r, g * c)


def _s5(u_tm, lam_re, lam_im, log_step, b_re, b_im, c_re, c_im, d, w_glu_bf16, bsz, seq):
    steps = min(S5_STEPS, seq)
    rows = steps * bsz
    n = S5_LANES
    lre = lam_re.reshape(1, n)
    lim = lam_im.reshape(1, n)
    lstep = jnp.repeat(log_step, S5_STATE).reshape(1, n)
    bre_big = _block_diag(jnp.transpose(b_re, (0, 2, 1)))
    bim_big = _block_diag(jnp.transpose(b_im, (0, 2, 1)))
    cre_big = _block_diag(jnp.transpose(c_re, (0, 2, 1)))
    cim_big = _block_diag(jnp.transpose(c_im, (0, 2, 1)))
    full = lambda shape: pl.BlockSpec(shape, lambda i: (0, 0))
    return pl.pallas_call(
        functools.partial(_s5_kernel, bsz=bsz, steps=steps),
        grid=(seq // steps,),
        in_specs=[
            pl.BlockSpec((rows, S5_WIDTH), lambda i: (i, 0)),
            full((1, n)), full((1, n)), full((1, n)),
            full((S5_WIDTH, n)), full((S5_WIDTH, n)),
            full((n, S5_WIDTH)), full((n, S5_WIDTH)),
            full((1, S5_WIDTH)), full((S5_WIDTH, S5_WIDTH)),
        ],
        out_specs=pl.BlockSpec((rows, S5_WIDTH), lambda i: (i, 0)),
        out_shape=jax.ShapeDtypeStruct((seq * bsz, S5_WIDTH), BF16),
        scratch_shapes=[
            pltpu.VMEM((S5_WIDTH, 2 * n), BF16),
            pltpu.VMEM((2 * n, S5_WIDTH), BF16),
            pltpu.VMEM((2, n), F32),
            pltpu.VMEM((bsz, 2 * n), F32),
            pltpu.VMEM((rows, 2 * n), F32),
        ],
        compiler_params=pltpu.CompilerParams(
            dimension_semantics=("arbitrary",),
            vmem_limit_bytes=VMEM_LIMIT_BYTES),
        name="s5",
    )(u_tm, lre, lim, lstep, bre_big, bim_big, cre_big, cim_big, d.reshape(1, S5_WIDTH), w_glu_bf16)


def _attn_kernel(q_ref, k_ref, v_ref, lq1_ref, lk1_ref, lq2_ref, lk2_ref, g_ref, o_ref,
                 vt_ref, vmax_ref, boff_ref, bdiag_ref, lhs_ref, m_ref, acc_ref, *, layer_idx, tq, seq):
    i = pl.program_id(1)
    tk = tq
    lam_init = 0.8 - 0.6 * math.exp(-0.3 * layer_idx)
    n_tiles = DIFF_WIDTH // LANES
    groups = LANES // DIFF_QKDIM
    slopes = [LOG2E * 2.0 ** (-8.0 * (h + 1) / DIFF_HEADS) for h in range(DIFF_HEADS)]

    @pl.when((pl.program_id(0) == 0) & (i == 0))
    def _():
        r = lax.broadcasted_iota(jnp.int32, (tk, tq), 0)
        c = lax.broadcasted_iota(jnp.int32, (tk, tq), 1)
        rel = (c - r).astype(F32)
        visible = (r // CHUNK) <= (c // CHUNK)
        for h in range(DIFF_HEADS):
            boff_ref[h] = -slopes[h] * rel
            bdiag_ref[h] = jnp.where(visible, -slopes[h] * jnp.abs(rel), MASK_VALUE)

    @pl.when(i == 0)
    def _():
        extra = lax.broadcasted_iota(jnp.int32, (ATTN_VT_ROWS - DIFF_VDIM, tk), 0)
        ones_row = jnp.where(extra == 0, 1.0, 0.0).astype(BF16)
        vmax_ref[...] = jnp.max(jnp.abs(v_ref[...].astype(F32)), axis=(0, 1), keepdims=True)
        for j in range(seq // tk):
            v_t = v_ref[j * tk:(j + 1) * tk, :].astype(F32).T.astype(BF16)
            for h in range(DIFF_HEADS):
                vt_ref[j, h, 0:DIFF_VDIM, :] = v_t[h * DIFF_VDIM:(h + 1) * DIFF_VDIM]
                vt_ref[j, h, DIFF_VDIM:ATTN_VT_ROWS, :] = ones_row

    lane = lax.broadcasted_iota(jnp.int32, (tq, LANES), 1)
    for t in range(n_tiles):
        qt = q_ref[:, t * LANES:(t + 1) * LANES]
        for g in range(groups):
            lhs_ref[t, g * tq:(g + 1) * tq, :] = jnp.where(lane // DIFF_QKDIM == g, qt, jnp.zeros_like(qt))

    def raw_scores(j, t):
        k0 = pl.multiple_of(j * tk, tk)
        kt = k_ref[pl.ds(k0, tk), t * LANES:(t + 1) * LANES]
        return lax.dot_general(kt, lhs_ref[t], _NT, preferred_element_type=F32)

    def pipelined(key_tiles, make_consume):
        items = [(j, t) for j in key_tiles for t in range(n_tiles)]
        consumers = [make_consume(j) for j in key_tiles]
        raw = raw_scores(*items[0])
        for n, (j, t) in enumerate(items):
            nxt = raw_scores(*items[n + 1]) if n + 1 < len(items) else None
            consumers[n // n_tiles](t, raw)
            raw = nxt

    def tile_step(j, bias_ref, first):
        gap = ((i - j) * tq).astype(F32)

        def consume(t, raw):
            for hh in range(2):
                head = 2 * t + hh
                off = -slopes[head] * gap
                p_parts = []
                alphas = []
                for c in range(2):
                    cols = slice((2 * hh + c) * tq, (2 * hh + c + 1) * tq)
                    mcols = slice(c * tq, (c + 1) * tq)
                    s = raw[:, cols] + bias_ref[head]
                    s_max = jnp.max(s, axis=0, keepdims=True)
                    if first:
                        m_new = s_max
                        shift = s_max
                    else:
                        m_old = m_ref[head, :, mcols]
                        m_new = jnp.maximum(m_old, s_max + off)
                        shift = m_new - off
                        alphas.append(jnp.exp2(m_old - m_new))
                    m_ref[head, :, mcols] = m_new
                    p_parts.append(jnp.exp2(s - shift).astype(BF16))
                pv = jnp.dot(vt_ref[j, head], jnp.concatenate(p_parts, axis=1),
                             preferred_element_type=F32)
                if first:
                    acc_ref[head] = pv
                else:
                    acc_ref[head] = jnp.concatenate(alphas, axis=1) * acc_ref[head] + pv

        pipelined([j], lambda _: consume)

    def fast_consume(j):
        gap = ((i - j) * tq).astype(F32)

        def consume(t, raw):
            for hh in range(2):
                head = 2 * t + hh
                row = -slopes[head] * gap - m_ref[head]
                bias = boff_ref[head]
                p_parts = []
                for c in range(2):
                    cols = slice((2 * hh + c) * tq, (2 * hh + c + 1) * tq)
                    p_parts.append(
                        jnp.exp2((raw[:, cols] + bias) + row[:, c * tq:(c + 1) * tq]).astype(BF16))
                acc_ref[head] += jnp.dot(vt_ref[j, head], jnp.concatenate(p_parts, axis=1),
                                         preferred_element_type=F32)

        return consume

    def fast_pair(jj, carry):
        pipelined([2 * jj, 2 * jj + 1], fast_consume)
        return carry

    def robust_step(j, carry):
        tile_step(j, boff_ref, False)
        return carry

    tile_step(i, bdiag_ref, True)
    lax.fori_loop(0, i // 2, fast_pair, 0)

    @pl.when(i % 2 == 1)
    def _():
        pipelined([i - 1], fast_consume)

    l_max = acc_ref[0, DIFF_VDIM:DIFF_VDIM + 1, :]
    for h in range(1, DIFF_HEADS):
        l_max = jnp.maximum(l_max, acc_ref[h, DIFF_VDIM:DIFF_VDIM + 1, :])
    bound = jnp.max(l_max, axis=1, keepdims=True) * vmax_ref[...]
    overflowed = jnp.logical_not(bound[0, 0] < ATTN_SAFE_SUM)

    @pl.when(overflowed)
    def _():
        tile_step(i, bdiag_ref, True)
        lax.fori_loop(0, i, robust_step, 0)

    lam = (jnp.exp(jnp.sum(lq1_ref[...] * lk1_ref[...], axis=-1, keepdims=True))
           - jnp.exp(jnp.sum(lq2_ref[...] * lk2_ref[...], axis=-1, keepdims=True))
           + lam_init)
    for t in range(n_tiles):
        halves = []
        for hh in range(2):
            head = 2 * t + hh
            inv = 1.0 / acc_ref[head, DIFF_VDIM:DIFF_VDIM + 1, :]
            o0 = acc_ref[head, 0:DIFF_VDIM, 0:tq] * inv[:, 0:tq]
            o1 = acc_ref[head, 0:DIFF_VDIM, tq:2 * tq] * (lam * inv[:, tq:2 * tq])
            o = o0 - o1
            ms = jnp.mean(o * o, axis=0, keepdims=True)
            halves.append(o * lax.rsqrt(ms + EPS))
        o_t = jnp.concatenate(halves, axis=0) * g_ref[t * LANES:(t + 1) * LANES, :]
        o_ref[:, t * LANES:(t + 1) * LANES] = (o_t.T * (1.0 - lam_init)).astype(BF16)


def _attn(qkv, lq1, lk1, lq2, lk2, subln_g, layer_idx, bsz, seq):
    tq = min(ATTN_TQ, seq)
    nq = seq // tq
    small = lambda: pl.BlockSpec((1, DIFF_QKDIM), lambda b, i: (0, 0))
    return pl.pallas_call(
        functools.partial(_attn_kernel, layer_idx=layer_idx, tq=tq, seq=seq),
        grid=(bsz, nq),
        in_specs=[
            pl.BlockSpec((tq, DIFF_WIDTH), lambda b, i: (b * nq + i, 0)),
            pl.BlockSpec((seq, DIFF_WIDTH), lambda b, i: (b, 1)),
            pl.BlockSpec((seq, DIFF_WIDTH), lambda b, i: (b, 2)),
            small(), small(), small(), small(),
            pl.BlockSpec((DIFF_WIDTH, 1), lambda b, i: (0, 0)),
        ],
        out_specs=pl.BlockSpec((tq, DIFF_WIDTH), lambda b, i: (b * nq + i, 0)),
        out_shape=jax.ShapeDtypeStruct((bsz * seq, DIFF_WIDTH), BF16),
        scratch_shapes=[
            pltpu.VMEM((seq // tq, DIFF_HEADS, ATTN_VT_ROWS, tq), BF16),
            pltpu.VMEM((1, 1), F32),
            pltpu.VMEM((DIFF_HEADS, tq, tq), F32),
            pltpu.VMEM((DIFF_HEADS, tq, tq), F32),
            pltpu.VMEM((DIFF_WIDTH // LANES, 4 * tq, LANES), BF16),
            pltpu.VMEM((DIFF_HEADS, 1, 2 * tq), F32),
            pltpu.VMEM((DIFF_HEADS, ATTN_VT_ROWS, 2 * tq), F32),
        ],
        compiler_params=pltpu.CompilerParams(
            dimension_semantics=("arbitrary", "arbitrary"),
            vmem_limit_bytes=VMEM_LIMIT_BYTES),
        name="attn",
    )(qkv, qkv, qkv, lq1.reshape(1, -1), lk1.reshape(1, -1), lq2.reshape(1, -1), lk2.reshape(1, -1),
      jnp.tile(subln_g, DIFF_HEADS).reshape(DIFF_WIDTH, 1))


def _hgrn_kernel(qq_ref, vv_ref, gs_ref, lf_ref, kk_ref, ng_ref, o_ref, st_ref, *, rows):
    w = HGRN_WIDTH
    c = CHUNK

    @pl.when(pl.program_id(1) == 0)
    def _():
        st_ref[...] = jnp.zeros_like(st_ref)

    row = lax.broadcasted_iota(jnp.int32, (c, w), 0)
    lane_w = lax.broadcasted_iota(jnp.int32, (HGRN_SUB, w), 1)
    bd_mask = (lax.broadcasted_iota(jnp.int32, (LANES, LANES), 0) // HGRN_DIM
               == lax.broadcasted_iota(jnp.int32, (LANES, LANES), 1) // HGRN_DIM)
    causal = (lax.broadcasted_iota(jnp.int32, (c, c), 1)
              <= lax.broadcasted_iota(jnp.int32, (c, c), 0))
    lane128 = lax.broadcasted_iota(jnp.int32, (c, LANES), 1)
    ones_bd = _head_ones(w, HGRN_DIM)
    n_sub = c // HGRN_SUB

    def prepare(ci):
        r0 = ci * c
        kk = kk_ref[r0:r0 + c, :]
        qq = qq_ref[r0:r0 + c, :].astype(F32)
        vv_b = vv_ref[r0:r0 + c, :]

        bc = lf_ref[r0:r0 + c, :]
        sh = 1
        while sh < c:
            bc = bc + jnp.where(row >= sh, pltpu.roll(bc, sh, axis=0), 0.0)
            sh *= 2

        blocks = []
        for si in range(n_sub):
            s0 = si * HGRN_SUB
            ref_row = bc[s0 + HGRN_SUB // 2:s0 + HGRN_SUB // 2 + 1, :]
            q_s = qq[s0:s0 + HGRN_SUB] * jnp.exp2(bc[s0:s0 + HGRN_SUB] - ref_row)
            n_keys = s0 + HGRN_SUB
            k_s = kk[0:n_keys] * jnp.exp2(ref_row - bc[0:n_keys])
            if n_keys < c:
                k_s = jnp.concatenate([k_s, jnp.zeros((c - n_keys, w), F32)], axis=0)
            k_s = k_s.astype(BF16)
            lhs = jnp.concatenate(
                [jnp.where(lane_w // HGRN_DIM == h, q_s, 0.0) for h in range(HGRN_HEADS)],
                axis=0).astype(BF16)
            blocks.append(lax.dot_general(lhs, k_s, _NT, preferred_element_type=F32))

        b_last = bc[c - 1:c, :]
        return dict(
            r0=r0, vv_b=vv_b, blocks=blocks,
            q_dec=(qq * jnp.exp2(bc)).astype(BF16),
            k2=(kk * jnp.exp2(b_last - bc)).astype(BF16),
            vv_t=vv_b.astype(F32).T.astype(BF16),
            carry_dec=jnp.exp2(b_last))

    def advance_state(ch):
        o_inter_tiles = []
        for t in range(w // LANES):
            tl = slice(t * LANES, (t + 1) * LANES)
            st = st_ref[t]
            o_inter_tiles.append(lax.dot_general(ch["q_dec"][:, tl], st.astype(BF16), _NT,
                                                 preferred_element_type=F32))
            upd = jnp.dot(ch["vv_t"][tl], ch["k2"][:, tl], preferred_element_type=F32)
            st_ref[t] = st * ch["carry_dec"][:, tl] + jnp.where(bd_mask, upd, 0.0)
        return jnp.concatenate(o_inter_tiles, axis=1)

    def finish(ch, o_inter):
        o_tiles = []
        for t in range(w // LANES):
            v_pair = ch["vv_b"][:, t * LANES:(t + 1) * LANES]
            res = []
            for hh in range(2):
                h = 2 * t + hh
                a_h = jnp.concatenate(
                    [ch["blocks"][si][h * HGRN_SUB:(h + 1) * HGRN_SUB] for si in range(n_sub)], axis=0)
                a_h = jnp.where(causal, a_h, 0.0).astype(BF16)
                res.append(jnp.dot(a_h, v_pair, preferred_element_type=F32))
            o_tiles.append(jnp.where(lane128 < HGRN_DIM, res[0], res[1]))
        o = jnp.concatenate(o_tiles, axis=1) + o_inter
        ms = _split_dot(o * o, ones_bd) * (1.0 / HGRN_DIM)
        o = o * lax.rsqrt(ms + EPS) * ng_ref[...]
        r0 = ch["r0"]
        o_ref[r0:r0 + c, :] = (o * gs_ref[r0:r0 + c, :].astype(F32)).astype(BF16)

    n_chunks = rows // c
    ch = prepare(0)
    for ci in range(n_chunks):
        nxt = prepare(ci + 1) if ci + 1 < n_chunks else None
        o_inter = advance_state(ch)
        finish(ch, o_inter)
        ch = nxt


def _hgrn(h_b, h_f, norm_g, bsz, seq):
    rows = min(HGRN_ROWS, seq)
    ns = seq // rows
    w = HGRN_WIDTH
    col = lambda j: pl.BlockSpec((rows, w), lambda b, s, j=j: (b * ns + s, j))
    return pl.pallas_call(
        functools.partial(_hgrn_kernel, rows=rows),
        grid=(bsz, ns),
        in_specs=[col(0), col(1), col(2), col(0), col(1),
                  pl.BlockSpec((1, w), lambda b, s: (0, 0))],
        out_specs=pl.BlockSpec((rows, w), lambda b, s: (b * ns + s, 0)),
        out_shape=jax.ShapeDtypeStruct((bsz * seq, w), BF16),
        scratch_shapes=[pltpu.VMEM((w // LANES, LANES, LANES), F32)],
        compiler_params=pltpu.CompilerParams(
            dimension_semantics=("arbitrary", "arbitrary"),
            vmem_limit_bytes=VMEM_LIMIT_BYTES),
        name="hgrn",
    )(h_b, h_b, h_b, h_f, h_f, jnp.tile(norm_g, HGRN_HEADS).reshape(1, w))


def _ffn_kernel(x_ref, ya_ref, yb_ref, yc_ref, wo_ref, g_ref, wup_ref, wgate_ref, cw_ref, cb_ref,
                wdown_ref, fg_ref, o_ref, up_ref, *, rows, final):
    s = pl.program_id(1)

    @pl.when(s == 0)
    def _():
        up_ref[0:SUBLANES, :] = jnp.zeros((SUBLANES, D_FF), F32)

    parts = [slice(p * (rows // FFN_PARTS), (p + 1) * (rows // FFN_PARTS)) for p in range(FFN_PARTS)]
    x1 = []
    for r in parts:
        mix = jnp.concatenate([ya_ref[r, :], yb_ref[r, :], yc_ref[r, :]], axis=-1).astype(BF16)
        x1.append(x_ref[r, :] + jnp.dot(mix, wo_ref[...], preferred_element_type=F32))
    gates = []
    for p, r in enumerate(parts):
        h = _rms(x1[p], g_ref[...]).astype(BF16)
        up_ref[SUBLANES + r.start:SUBLANES + r.stop, :] = jnp.dot(h, wup_ref[...],
                                                                  preferred_element_type=F32)
        gates.append(jnp.dot(h, wgate_ref[...], preferred_element_type=F32))
    for p, r in enumerate(parts):
        conv = (cw_ref[2:3, :] * up_ref[SUBLANES + r.start:SUBLANES + r.stop, :]
                + cw_ref[1:2, :] * up_ref[SUBLANES - 1 + r.start:SUBLANES - 1 + r.stop, :]
                + cw_ref[0:1, :] * up_ref[SUBLANES - 2 + r.start:SUBLANES - 2 + r.stop, :]
                + cb_ref[...])
        act = (_gelu_tanh(conv) * gates[p]).astype(BF16)
        x2 = x1[p] + jnp.dot(act, wdown_ref[...], preferred_element_type=F32)
        if final:
            x2 = _rms(x2, fg_ref[...])
        o_ref[r, :] = x2
    up_ref[0:SUBLANES, :] = up_ref[rows:rows + SUBLANES, :]


def _ffn(x2d, ya_tm, yb, yc, w_out, g, w_up, w_gate, conv_w, conv_b, w_down, final_g, final, bsz, seq):
    rows = min(FFN_ROWS, seq)
    ns = seq // rows
    const = lambda shape: pl.BlockSpec(shape, lambda b, s: (0, 0), pipeline_mode=pl.Buffered(1))
    tok = lambda width: pl.BlockSpec((rows, width), lambda b, s: (b * ns + s, 0))
    return pl.pallas_call(
        functools.partial(_ffn_kernel, rows=rows, final=final),
        grid=(bsz, ns),
        in_specs=[
            tok(D_MODEL),
            pl.BlockSpec((rows, S5_WIDTH), lambda b, s: (s, b)),
            tok(DIFF_WIDTH), tok(HGRN_WIDTH),
            const((D_MODEL, D_MODEL)), const((1, D_MODEL)),
            const((D_MODEL, D_FF)), const((D_MODEL, D_FF)),
            const((3, D_FF)), const((1, D_FF)),
            const((D_FF, D_MODEL)), const((1, D_MODEL)),
        ],
        out_specs=tok(D_MODEL),
        out_shape=jax.ShapeDtypeStruct((bsz * seq, D_MODEL), F32),
        scratch_shapes=[pltpu.VMEM((SUBLANES + rows, D_FF), F32)],
        compiler_params=pltpu.CompilerParams(
            dimension_semantics=("arbitrary", "arbitrary"),
            vmem_limit_bytes=VMEM_LIMIT_BYTES),
        name="ffn",
    )(x2d, ya_tm, yb, yc, w_out, g, w_up, w_gate, conv_w, conv_b, w_down, final_g)


def kernel(x, norm_mix_g, w_in, s5_lambda_re, s5_lambda_im, s5_log_step, s5_b_re, s5_b_im, s5_c_re,
           s5_c_im, s5_d, s5_w_glu, diff_lambda_q1, diff_lambda_k1, diff_lambda_q2, diff_lambda_k2,
           diff_subln_g, hgrn_lb_logits, hgrn_norm_g, w_out, norm_ffn_g, w_up, w_gate, conv_w, conv_b,
           w_down, final_norm_g):
    bsz, seq, d = x.shape
    assert d == D_MODEL and seq % CHUNK == 0 and bsz % SUBLANES == 0
    x2d = x.reshape(bsz * seq, d)
    for l in range(DEPTH):
        u_tm, qkv, h_b, h_f = _proj(x2d, norm_mix_g[l].reshape(1, d), w_in[l].astype(BF16),
                                    hgrn_lb_logits, l, bsz, seq)
        ya_tm = _s5(u_tm.reshape(seq * bsz, S5_WIDTH), s5_lambda_re[l], s5_lambda_im[l], s5_log_step[l],
                    s5_b_re[l], s5_b_im[l], s5_c_re[l], s5_c_im[l], s5_d[l], s5_w_glu[l].astype(BF16),
                    bsz, seq)
        yb = _attn(qkv, diff_lambda_q1[l], diff_lambda_k1[l], diff_lambda_q2[l], diff_lambda_k2[l],
                   diff_subln_g[l], l, bsz, seq)
        yc = _hgrn(h_b, h_f, hgrn_norm_g[l], bsz, seq)
        x2d = _ffn(x2d, ya_tm.reshape(seq, bsz * S5_WIDTH), yb, yc, w_out[l].astype(BF16),
                   norm_ffn_g[l].reshape(1, d), w_up[l].astype(BF16), w_gate[l].astype(BF16),
                   conv_w[l], conv_b[l].reshape(1, D_FF), w_down[l].astype(BF16),
                   final_norm_g.reshape(1, d), l == DEPTH - 1, bsz, seq)
    return x2d.reshape(bsz, seq, d)
```
